```python
import math
import jax, jax.numpy as jnp
from jax import lax
import numpy as np

D_MODEL = 1024
BATCH = 8
SEQ = 2048
DEPTH = 4
DEC_BATCH = 128
DEC_SEQ = 4
PAST_LEN = 2048
PAGE_SIZE = 128

D_SSM = D_MODEL // 2
SSM_HEAD_DIM = 64
SSM_HEADS = D_SSM // SSM_HEAD_DIM
SSM_GROUPS = 2
SSM_STATE = 64
CONV_W = 4
CONV_DIM = D_SSM + 2 * SSM_GROUPS * SSM_STATE
SSD_CHUNK = 128

D_ATT = D_MODEL - D_SSM
HEAD_DIM = 64
ATT_HEADS = D_ATT // HEAD_DIM
N_KV = 2
HPG = ATT_HEADS // N_KV
CMP_BLOCK = 32
SEL_BLOCK = 64
TOP_N = 16
WINDOW = 512
SEL_QBLOCK = 64
WIN_QBLOCK = 128
FORCE_SCORE = 1.0e4

MEM_LEN = 256
MEM_HEADS = 4
MEM_HEAD_DIM = D_MODEL // MEM_HEADS
D_FF = 4 * D_MODEL
EPS = 1e-6

IN_DIM = D_SSM + CONV_DIM + SSM_HEADS + D_ATT + 6 * N_KV * HEAD_DIM + 3 * ATT_HEADS
D_MIX = D_SSM + D_ATT

kernel_name = 'hybrid_ssd_nsa_decoder_step'


def rmsnorm(x, g):
    xf = x.astype(jnp.float32)
    y = xf * lax.rsqrt(jnp.mean(xf * xf, axis=-1, keepdims=True) + EPS)
    return (y * g.astype(jnp.float32)).astype(x.dtype)


def masked_softmax(s, mask):
    s = jnp.where(mask, s.astype(jnp.float32), -jnp.inf)
    m = jnp.max(s, axis=-1, keepdims=True)
    m = jnp.where(jnp.isfinite(m), m, 0.0)
    e = jnp.exp(s - m)
    d = jnp.sum(e, axis=-1, keepdims=True)
    return e / jnp.where(d > 0, d, 1.0)


def causal_conv_silu(x_new, conv_prev, w, b):
    xf = jnp.concatenate([conv_prev.astype(x_new.dtype), x_new], axis=1)
    y = lax.conv_general_dilated(xf, w[:, None, :].astype(xf.dtype), window_strides=(1,), padding='VALID',
                                 dimension_numbers=('NWC', 'WIO', 'NWC'), feature_group_count=xf.shape[-1])
    return jax.nn.silu(y + b), xf[:, -(CONV_W - 1):]


def ssd_scan(x, dt, a, bm, cm, h0, chunk):
    nb, L, H, P = x.shape
    N = bm.shape[-1]
    nc = L // chunk
    x = x.astype(jnp.float32).reshape(nb, nc, chunk, H, P)
    dt = dt.reshape(nb, nc, chunk, H)
    bm = bm.astype(jnp.float32).reshape(nb, nc, chunk, H, N)
    cm = cm.astype(jnp.float32).reshape(nb, nc, chunk, H, N)
    acs = jnp.cumsum(dt * a, axis=2)
    seg = acs[:, :, :, None, :] - acs[:, :, None, :, :]
    causal = jnp.tril(jnp.ones((chunk, chunk), bool))[None, None, :, :, None]
    decay_ij = jnp.exp(jnp.where(causal, seg, -jnp.inf))
    xdt = x * dt[..., None]
    scores = jnp.einsum('bcihn,bcjhn->bcijh', cm, bm) * decay_ij
    y_diag = jnp.einsum('bcijh,bcjhp->bcihp', scores, xdt)
    decay_end = jnp.exp(acs[:, :, -1:, :] - acs)
    chunk_states = jnp.einsum('bcqhn,bcqh,bcqhp->bchpn', bm, decay_end, xdt)
    chunk_decay = jnp.exp(acs[:, :, -1, :])

    def step(h, inp):
        st, dec = inp
        return h * dec[:, :, None, None] + st, h

    h_final, h_in = lax.scan(step, h0.astype(jnp.float32),
                             (jnp.moveaxis(chunk_states, 1, 0), jnp.moveaxis(chunk_decay, 1, 0)))
    h_in = jnp.moveaxis(h_in, 0, 1)
    y_off = jnp.einsum('bcqhn,bchpn,bcqh->bcqhp', cm, h_in, jnp.exp(acs))
    return (y_diag + y_off).reshape(nb, L, H, P), h_final


def ssd_branch(z, xbc, dt_raw, conv_prev, h0, lp):
    nb, L, _ = xbc.shape
    xbc_c, conv_new = causal_conv_silu(xbc, conv_prev, lp['conv_w'], lp['conv_b'])
    xs, bm, cm = jnp.split(xbc_c, [D_SSM, D_SSM + SSM_GROUPS * SSM_STATE], axis=-1)
    xs = xs.reshape(nb, L, SSM_HEADS, SSM_HEAD_DIM)
    rep = SSM_HEADS // SSM_GROUPS
    bm = jnp.repeat(bm.reshape(nb, L, SSM_GROUPS, SSM_STATE), rep, axis=2)
    cm = jnp.repeat(cm.reshape(nb, L, SSM_GROUPS, SSM_STATE), rep, axis=2)
    dt = jax.nn.softplus(dt_raw.astype(jnp.float32) + lp['dt_bias'].astype(jnp.float32))
    a = -jnp.exp(lp['a_log'].astype(jnp.float32))
    chunk = SSD_CHUNK if L % SSD_CHUNK == 0 else L
    y, h_new = ssd_scan(xs, dt, a, bm, cm, h0, chunk)
    y = y + lp['d_skip'].astype(jnp.float32)[:, None] * xs.astype(jnp.float32)
    y = y.reshape(nb, L, D_SSM) * jax.nn.silu(z.astype(jnp.float32))
    return rmsnorm(y, lp['ssm_norm']).astype(z.dtype), conv_new, h_new


def nsa_cmp_sel(q, q_start, kc, vc, ks, vs, wk_cmp, wv_cmp):
    nb, lq = q.shape[0], q.shape[1]
    lk = kc.shape[1]
    lp = -(-lk // SEL_BLOCK) * SEL_BLOCK
    pad = ((0, 0), (0, lp - lk), (0, 0), (0, 0))
    kc, vc, ks, vs = (jnp.pad(t, pad) for t in (kc, vc, ks, vs))
    n_cmp = lp // CMP_BLOCK
    n_sel = lp // SEL_BLOCK
    scale = HEAD_DIM ** -0.5
    qg = q.reshape(nb, lq, N_KV, HPG, HEAD_DIM)
    qpos = q_start + jnp.arange(lq)
    k_cmp = jnp.einsum('bnjgd,jd->bngd', kc.reshape(nb, n_cmp, CMP_BLOCK, N_KV, HEAD_DIM), wk_cmp)
    v_cmp = jnp.einsum('bnjgd,jd->bngd', vc.reshape(nb, n_cmp, CMP_BLOCK, N_KV, HEAD_DIM), wv_cmp)
    s = jnp.einsum('bqgcd,bngd->bgcqn', qg, k_cmp) * scale
    cmask = ((jnp.arange(n_cmp) + 1) * CMP_BLOCK - 1)[None, :] <= qpos[:, None]
    p_cmp = masked_softmax(s, cmask)
    o_cmp = jnp.einsum('bgcqn,bngd->bqgcd', p_cmp, v_cmp).reshape(nb, lq, ATT_HEADS, HEAD_DIM)
    imp = p_cmp.sum(axis=2).reshape(nb, N_KV, lq, n_sel, SEL_BLOCK // CMP_BLOCK).sum(-1)
    blk = jnp.arange(n_sel)[None, :]
    cur = (qpos // SEL_BLOCK)[:, None]
    forced = (blk == 0) | (blk == cur) | (blk == cur - 1)
    score = jnp.where(blk <= cur, jnp.where(forced, FORCE_SCORE, imp), -jnp.inf)
    top_s, top_i = lax.top_k(score, min(TOP_N, n_sel))
    top_ok = jnp.isfinite(top_s)
    n_top = top_i.shape[-1]
    qb = SEL_QBLOCK if lq % SEL_QBLOCK == 0 else lq
    nqb = lq // qb
    ksb = ks.reshape(nb, n_sel, SEL_BLOCK, N_KV, HEAD_DIM).transpose(0, 3, 1, 2, 4)
    vsb = vs.reshape(nb, n_sel, SEL_BLOCK, N_KV, HEAD_DIM).transpose(0, 3, 1, 2, 4)
    gather = jax.vmap(jax.vmap(lambda t, i: t[i]))

    def sel_block(args):
        qblk, idx, ok, pos = args
        kg = gather(ksb, idx)
        vg = gather(vsb, idx)
        sc = jnp.einsum('bqgcd,bgqnrd->bgcqnr', qblk, kg) * scale
        kpos = idx[..., None] * SEL_BLOCK + jnp.arange(SEL_BLOCK)
        m = ok[..., None] & (kpos <= pos[None, None, :, None, None])
        p = masked_softmax(sc.reshape(nb, N_KV, HPG, qb, n_top * SEL_BLOCK),
                           m.reshape(nb, N_KV, 1, qb, n_top * SEL_BLOCK))
        return jnp.einsum('bgcqnr,bgqnrd->bqgcd', p.reshape(nb, N_KV, HPG, qb, n_top, SEL_BLOCK), vg)

    xs = (jnp.moveaxis(qg.reshape(nb, nqb, qb, N_KV, HPG, HEAD_DIM), 1, 0),
          jnp.moveaxis(top_i.reshape(nb, N_KV, nqb, qb, n_top), 2, 0),
          jnp.moveaxis(top_ok.reshape(nb, N_KV, nqb, qb, n_top), 2, 0),
          qpos.reshape(nqb, qb))
    o_sel = jnp.moveaxis(lax.map(sel_block, xs), 0, 1).reshape(nb, lq, ATT_HEADS, HEAD_DIM)
    return o_cmp, o_sel


def window_attn(q, kw, vw, q_start, k_start):
    nb, lq = q.shape[0], q.shape[1]
    qb = WIN_QBLOCK if lq % WIN_QBLOCK == 0 else lq
    nqb = lq // qb
    span = qb + WINDOW
    pad = ((0, 0), (WINDOW, 0), (0, 0), (0, 0))
    idx = (q_start - k_start) + jnp.arange(nqb)[:, None] * qb + jnp.arange(span)[None, :]
    kb = jnp.pad(kw, pad)[:, idx]
    vb = jnp.pad(vw, pad)[:, idx]
    kpos = (k_start - WINDOW) + idx
    qpos = (q_start + jnp.arange(lq)).reshape(nqb, qb)
    d = qpos[:, :, None] - kpos[:, None, :]
    mask = (d >= 0) & (d < WINDOW) & (kpos[:, None, :] >= k_start)
    qg = q.reshape(nb, nqb, qb, N_KV, HPG, HEAD_DIM)
    s = jnp.einsum('bnqgcd,bnkgd->bngcqk', qg, kb) * (HEAD_DIM ** -0.5)
    p = masked_softmax(s, mask[None, :, None, None])
    return jnp.einsum('bngcqk,bnkgd->bnqgcd', p, vb).reshape(nb, lq, ATT_HEADS, HEAD_DIM)


def hybrid_mixer(h, lp, conv_prev, ssm_prev, kv_past, win_past, q_start):
    nb, L, _ = h.shape
    s1 = D_SSM
    s2 = s1 + CONV_DIM
    s3 = s2 + SSM_HEADS
    s4 = s3 + D_ATT
    s5 = s4 + 6 * N_KV * HEAD_DIM
    z, xbc, dt_raw, q, kv, gates = jnp.split(h @ lp['w_in'], [s1, s2, s3, s4, s5], axis=-1)
    y_ssd, conv_new, ssm_new = ssd_branch(z, xbc, dt_raw, conv_prev, ssm_prev, lp)
    q = q.reshape(nb, L, ATT_HEADS, HEAD_DIM)
    kv = kv.reshape(nb, L, 6, N_KV, HEAD_DIM)
    kv_rows, win_rows = kv[:, :, :4], kv[:, :, 4:]
    if kv_past is None:
        kv_all, win_all, keep = kv_rows, win_rows, min(WINDOW, L)
    else:
        kv_all = jnp.concatenate([kv_past.astype(kv.dtype), kv_rows], axis=1)
        win_all = jnp.concatenate([win_past.astype(kv.dtype), win_rows], axis=1)
        keep = win_past.shape[1]
    k_start = q_start - (win_all.shape[1] - L)
    o_cmp, o_sel = nsa_cmp_sel(q, q_start, kv_all[:, :, 0], kv_all[:, :, 1], kv_all[:, :, 2], kv_all[:, :, 3],
                               lp['cmp_wk'], lp['cmp_wv'])
    o_win = window_attn(q, win_all[:, :, 0], win_all[:, :, 1], q_start, k_start)
    g = jax.nn.sigmoid((gates + lp['gate_b']).astype(jnp.float32)).reshape(nb, L, 3, ATT_HEADS, 1)
    o_att = g[:, :, 0] * o_cmp + g[:, :, 1] * o_sel + g[:, :, 2] * o_win
    o_att = o_att.reshape(nb, L, D_ATT).astype(h.dtype)
    out = jnp.concatenate([y_ssd, o_att], axis=-1) @ lp['w_out']
    return out, conv_new, ssm_new, kv_rows, win_all[:, -keep:]


def mem_kv(mem, g, w_k, w_v):
    nb, m, _ = mem.shape
    mn = rmsnorm(mem, g)
    k = (mn @ w_k).reshape(nb, m, MEM_HEADS, MEM_HEAD_DIM)
    v = (mn @ w_v).reshape(nb, m, MEM_HEADS, MEM_HEAD_DIM)
    return jnp.stack([k, v], axis=2)


def mem_attn(h, kv, w_q, w_o):
    nb, L, _ = h.shape
    q = (h @ w_q).reshape(nb, L, MEM_HEADS, MEM_HEAD_DIM)
    s = jnp.einsum('blhd,bmhd->bhlm', q, kv[:, :, 0].astype(q.dtype)) * (MEM_HEAD_DIM ** -0.5)
    p = jax.nn.softmax(s.astype(jnp.float32), axis=-1)
    o = jnp.einsum('bhlm,bmhd->blhd', p, kv[:, :, 1].astype(jnp.float32))
    return o.reshape(nb, L, D_MODEL).astype(h.dtype) @ w_o


def layer_fwd(x, lp, mkv, conv_prev, ssm_prev, kv_past, win_past, q_start):
    m, conv_new, ssm_new, kv_rows, win_new = hybrid_mixer(rmsnorm(x, lp['norm_mix_pre']), lp, conv_prev, ssm_prev,
                                                          kv_past, win_past, q_start)
    x = x + rmsnorm(m, lp['norm_mix_post'])
    c = mem_attn(rmsnorm(x, lp['norm_mem_pre']), mkv, lp['w_mem_q'], lp['w_mem_o'])
    x = x + rmsnorm(c, lp['norm_mem_post'])
    f = jnp.square(jax.nn.relu(rmsnorm(x, lp['norm_ffn_pre']) @ lp['w_up'])) @ lp['w_down']
    x = x + rmsnorm(f, lp['norm_ffn_post'])
    return x, conv_new, ssm_new, kv_rows, win_new


def setup_inputs(seed: int = 0) -> dict:
    key = jax.random.key(seed)
    keys = iter(jax.random.split(key, 40))
    f32 = jnp.float32

    def nrm(shape, scale=1.0):
        return scale * jax.random.normal(next(keys), shape, f32)

    def gain(shape):
        return 1.0 + 0.02 * jax.random.normal(next(keys), shape, f32)

    n_pages = PAST_LEN // PAGE_SIZE
    n_used = DEC_BATCH * n_pages
    n_pool = n_used + -(-n_used // 4)
    win_keep = min(WINDOW, PAST_LEN)
    page_table = jax.random.permutation(next(keys), n_pool)[:n_used].reshape(DEC_BATCH, n_pages).astype(jnp.int32)
    dt0 = jnp.exp(jax.random.uniform(next(keys), (DEPTH, SSM_HEADS), f32, math.log(1e-3), math.log(1e-1)))
    return {
        'x_prompt': nrm((BATCH, SEQ, D_MODEL)),
        'x_sample': nrm((DEC_BATCH, DEC_SEQ, D_MODEL)),
        'mem_prompt': nrm((BATCH, MEM_LEN, D_MODEL)),
        'cache_nsa_kv': nrm((DEPTH, n_pool, PAGE_SIZE, 4, N_KV, HEAD_DIM)),
        'cache_mem_kv': nrm((DEPTH, DEC_BATCH, MEM_LEN, 2, MEM_HEADS, MEM_HEAD_DIM)),
        'state_nsa_win': nrm((DEPTH, DEC_BATCH, win_keep, 2, N_KV, HEAD_DIM)),
        'state_ssm': nrm((DEPTH, DEC_BATCH, SSM_HEADS, SSM_HEAD_DIM, SSM_STATE), 0.5),
        'state_conv': nrm((DEPTH, DEC_BATCH, CONV_W - 1, CONV_DIM)),
        'page_table': page_table,
        'norm_mix_pre': gain((DEPTH, D_MODEL)),
        'w_in': nrm((DEPTH, D_MODEL, IN_DIM), D_MODEL ** -0.5),
        'conv_w': nrm((DEPTH, CONV_W, CONV_DIM), CONV_W ** -0.5),
        'conv_b': nrm((DEPTH, CONV_DIM), 0.01),
        'dt_bias': dt0 + jnp.log(-jnp.expm1(-dt0)),
        'a_log': jnp.log(jax.random.uniform(next(keys), (DEPTH, SSM_HEADS), f32, 1.0, 16.0)),
        'd_skip': 1.0 + nrm((DEPTH, SSM_HEADS), 0.1),
        'ssm_norm': gain((DEPTH, D_SSM)),
        'cmp_wk': (CMP_BLOCK ** -0.5) * (1.0 + nrm((DEPTH, CMP_BLOCK, HEAD_DIM), 0.1)),
        'cmp_wv': (CMP_BLOCK ** -0.5) * (1.0 + nrm((DEPTH, CMP_BLOCK, HEAD_DIM), 0.1)),
        'gate_b': nrm((DEPTH, 3 * ATT_HEADS), 0.01),
        'w_out': nrm((DEPTH, D_MIX, D_MODEL), D_MIX ** -0.5),
        'norm_mix_post': gain((DEPTH, D_MODEL)),
        'norm_mem_src': gain((DEPTH, D_MODEL)),
        'w_mem_q': nrm((DEPTH, D_MODEL, D_MODEL), D_MODEL ** -0.5),
        'w_mem_k': nrm((DEPTH, D_MODEL, D_MODEL), D_MODEL ** -0.5),
        'w_mem_v': nrm((DEPTH, D_MODEL, D_MODEL), D_MODEL ** -0.5),
        'w_mem_o': nrm((DEPTH, D_MODEL, D_MODEL), D_MODEL ** -0.5),
        'norm_mem_pre': gain((DEPTH, D_MODEL)),
        'norm_mem_post': gain((DEPTH, D_MODEL)),
        'norm_ffn_pre': gain((DEPTH, D_MODEL)),
        'w_up': nrm((DEPTH, D_MODEL, D_FF), D_MODEL ** -0.5),
        'w_down': nrm((DEPTH, D_FF, D_MODEL), D_FF ** -0.5),
        'norm_ffn_post': gain((DEPTH, D_MODEL)),
    }


def reference(x_prompt, x_sample, mem_prompt, cache_nsa_kv, cache_mem_kv, state_nsa_win, state_ssm, state_conv,
              page_table, norm_mix_pre, w_in, conv_w, conv_b, dt_bias, a_log, d_skip, ssm_norm, cmp_wk, cmp_wv,
              gate_b, w_out, norm_mix_post, norm_mem_src, w_mem_q, w_mem_k, w_mem_v, w_mem_o, norm_mem_pre,
              norm_mem_post, norm_ffn_pre, w_up, w_down, norm_ffn_post):
    n_seq = page_table.shape[0]
    n_pb = x_prompt.shape[0]
    xp, xs = x_prompt, x_sample
    kv_p, kv_s, win_p, win_s, ssm_p, ssm_s, conv_p, conv_s, mkv_out = [], [], [], [], [], [], [], [], []
    for l in range(DEPTH):
        lp = {'norm_mix_pre': norm_mix_pre[l], 'w_in': w_in[l], 'conv_w': conv_w[l], 'conv_b': conv_b[l],
              'dt_bias': dt_bias[l], 'a_log': a_log[l], 'd_skip': d_skip[l], 'ssm_norm': ssm_norm[l],
              'cmp_wk': cmp_wk[l], 'cmp_wv': cmp_wv[l], 'gate_b': gate_b[l], 'w_out': w_out[l],
              'norm_mix_post': norm_mix_post[l], 'w_mem_q': w_mem_q[l], 'w_mem_o': w_mem_o[l],
              'norm_mem_pre': norm_mem_pre[l], 'norm_mem_post': norm_mem_post[l],
              'norm_ffn_pre': norm_ffn_pre[l], 'w_up': w_up[l], 'w_down': w_down[l],
              'norm_ffn_post': norm_ffn_post[l]}
        mkv = mem_kv(mem_prompt, norm_mem_src[l], w_mem_k[l], w_mem_v[l])
        conv0 = jnp.zeros((n_pb, CONV_W - 1, CONV_DIM), xp.dtype)
        h0 = jnp.zeros((n_pb, SSM_HEADS, SSM_HEAD_DIM, SSM_STATE), jnp.float32)
        xp, c_new, s_new, kv_new, w_new = layer_fwd(xp, lp, mkv, conv0, h0, None, None, 0)
        kv_p.append(kv_new)
        win_p.append(w_new)
        ssm_p.append(s_new.astype(state_ssm.dtype))
        conv_p.append(c_new.astype(state_conv.dtype))
        mkv_out.append(mkv)
        past = cache_nsa_kv[l][page_table].reshape(n_seq, PAST_LEN, 4, N_KV, HEAD_DIM)
        xs, c_new, s_new, kv_new, w_new = layer_fwd(xs, lp, cache_mem_kv[l], state_conv[l], state_ssm[l],
                                                    past, state_nsa_win[l], PAST_LEN)
        kv_s.append(kv_new)
        win_s.append(w_new)
        ssm_s.append(s_new.astype(state_ssm.dtype))
        conv_s.append(c_new.astype(state_conv.dtype))
    return (xp, xs, jnp.stack(kv_p), jnp.stack(kv_s), jnp.stack(win_p), jnp.stack(win_s),
            jnp.stack(ssm_p), jnp.stack(ssm_s), jnp.stack(conv_p), jnp.stack(conv_s), jnp.stack(mkv_out))
```

```python
import functools

import jax
import jax.numpy as jnp
from jax import lax
from jax.experimental import pallas as pl
from jax.experimental.pallas import tpu as pltpu

f32 = jnp.float32
bf16 = jnp.bfloat16
NEG = float("-inf")

SSM_HEAD_DIM = 64
SSM_GROUPS = 2
SSM_STATE = 64
SSD_CHUNK = 128
HEAD_DIM = 64
N_KV = 2
CMP_BLOCK = 32
SEL_BLOCK = 64
TOP_N = 16
WINDOW = 512
FORCE_SCORE = 1.0e4
MEM_HEADS = 4
EPS = 1e-6

LANES = 128
V7X_VMEM_BYTES = 64 * 1024 * 1024
SEL_TK = 512


def _params(sem, vmem_mb):
    assert vmem_mb * 1024 * 1024 < V7X_VMEM_BYTES
    return pltpu.CompilerParams(dimension_semantics=sem, vmem_limit_bytes=vmem_mb * 1024 * 1024)


def _nn(a, b):
    return jnp.dot(a, b, preferred_element_type=f32)


def _nt(a, b):
    return lax.dot_general(a, b, (((1,), (1,)), ((), ())), preferred_element_type=f32)


def _tn(a, b):
    return lax.dot_general(a, b, (((0,), (0,)), ((), ())), preferred_element_type=f32)


def _rms(x, g):
    return x * lax.rsqrt(jnp.mean(x * x, axis=-1, keepdims=True) + EPS) * g


def _masked_softmax(s, mask, axis=-1):
    s = jnp.where(mask, s, NEG)
    m = jnp.max(s, axis=axis, keepdims=True)
    m = jnp.where(jnp.isfinite(m), m, 0.0)
    e = jnp.exp(s - m)
    d = jnp.sum(e, axis=axis, keepdims=True)
    return e * (1.0 / jnp.where(d > 0, d, 1.0))


def _split2(x):
    hi = x.astype(bf16)
    lo = (x - hi.astype(f32)).astype(bf16)
    return hi, lo


def _split3(x):
    hi = x.astype(bf16)
    r = x - hi.astype(f32)
    mid = r.astype(bf16)
    lo = (r - mid.astype(f32)).astype(bf16)
    return hi, mid, lo


def _norm_proj(x, g, ws, wts=(), seq=None, tm=512):
    T, D = x.shape
    tm = min(tm, T)
    assert T % tm == 0
    ws, wts = list(ws), list(wts)
    n_row = len(ws)
    tps = (seq // tm) if wts else 1

    def body(x_ref, g_ref, *refs):
        w_refs, o_refs = refs[:len(ws) + len(wts)], refs[len(ws) + len(wts):]
        xn = _rms(x_ref[...], g_ref[...]).astype(bf16)
        for k in range(n_row):
            o_refs[k][...] = _nn(xn, w_refs[k][...])
        for k in range(n_row, len(w_refs)):
            o_refs[k][...] = _nt(w_refs[k][...], xn)

    in_specs = [pl.BlockSpec((tm, D), lambda i: (i, 0)), pl.BlockSpec((1, D), lambda i: (0, 0))]
    in_specs += [pl.BlockSpec(w.shape, lambda i: (0, 0)) for w in ws + wts]
    out_specs = [pl.BlockSpec((tm, w.shape[1]), lambda i: (i, 0)) for w in ws]
    out_specs += [pl.BlockSpec((None, w.shape[0], tm), lambda i: (i // tps, 0, i % tps)) for w in wts]
    out_shape = [jax.ShapeDtypeStruct((T, w.shape[1]), f32) for w in ws]
    out_shape += [jax.ShapeDtypeStruct((T // seq, w.shape[0], seq), f32) for w in wts]
    return pl.pallas_call(body, grid=(T // tm,), in_specs=in_specs, out_specs=out_specs, out_shape=out_shape,
                          compiler_params=_params(("parallel",), 48), name="norm_proj")(x, g.reshape(1, D), *ws, *wts)


def _linear_post(res, acts, ws, g, tm=512):
    T, D = res.shape
    tm = min(tm, T)
    n = len(acts)

    def body(*refs):
        a_refs, w_refs = refs[:n], refs[n:2 * n]
        g_ref, res_ref, o_ref = refs[2 * n:]
        acc = _nn(a_refs[0][...].astype(bf16), w_refs[0][...])
        for k in range(1, n):
            acc += _nn(a_refs[k][...].astype(bf16), w_refs[k][...])
        o_ref[...] = res_ref[...] + _rms(acc, g_ref[...])

    in_specs = [pl.BlockSpec((tm, a.shape[1]), lambda i: (i, 0)) for a in acts]
    in_specs += [pl.BlockSpec(w.shape, lambda i: (0, 0)) for w in ws]
    in_specs += [pl.BlockSpec((1, D), lambda i: (0, 0)), pl.BlockSpec((tm, D), lambda i: (i, 0))]
    return pl.pallas_call(body, grid=(T // tm,), in_specs=in_specs, out_specs=pl.BlockSpec((tm, D), lambda i: (i, 0)),
                          out_shape=jax.ShapeDtypeStruct((T, D), f32),
                          compiler_params=_params(("parallel",), 40), name="linear_post")(*acts, *ws, g.reshape(1, D), res)


def _ffn(x, g1, wu, wd, g2, tm, tc=512):
    T, D = x.shape
    F = wu.shape[1]
    tm = min(tm, T)
    nj = F // tc

    def body(x_ref, g1_ref, wu_ref, wd_ref, g2_ref, o_ref, xn_s, acc_s):
        j = pl.program_id(1)

        @pl.when(j == 0)
        def _():
            xn_s[...] = _rms(x_ref[...], g1_ref[...]).astype(bf16)
            acc_s[...] = jnp.zeros_like(acc_s)

        h = _nn(xn_s[...], wu_ref[...])
        h = jnp.square(jnp.maximum(h, 0.0))
        acc_s[...] += _nn(h.astype(bf16), wd_ref[...])

        @pl.when(j == nj - 1)
        def _():
            o_ref[...] = x_ref[...] + _rms(acc_s[...], g2_ref[...])

    return pl.pallas_call(
        body, grid=(T // tm, nj),
        in_specs=[pl.BlockSpec((tm, D), lambda i, j: (i, 0)), pl.BlockSpec((1, D), lambda i, j: (0, 0)),
                  pl.BlockSpec((D, tc), lambda i, j: (0, j)), pl.BlockSpec((tc, D), lambda i, j: (j, 0)),
                  pl.BlockSpec((1, D), lambda i, j: (0, 0))],
        out_specs=pl.BlockSpec((tm, D), lambda i, j: (i, 0)),
        out_shape=jax.ShapeDtypeStruct((T, D), f32),
        scratch_shapes=[pltpu.VMEM((tm, D), bf16), pltpu.VMEM((tm, D), f32)],
        compiler_params=_params(("parallel", "arbitrary"), 48), name="ffn")(x, g1.reshape(1, D), wu, wd, g2.reshape(1, D))


def _mem_attn_prompt(q, mkv, seq, tq=512):
    T, D = q.shape
    M = mkv.shape[0] // (T // seq)
    hd = D // MEM_HEADS
    scale = hd ** -0.5
    tps = seq // tq

    def body(q_ref, kv_ref, o_ref):
        for h in range(MEM_HEADS):
            qh = (q_ref[:, h * hd:(h + 1) * hd] * scale).astype(bf16)
            kh = kv_ref[:, h * hd:(h + 1) * hd].astype(bf16)
            vh = kv_ref[:, D + h * hd:D + (h + 1) * hd].astype(bf16)
            s = _nt(qh, kh)
            e = jnp.exp(s - jnp.max(s, axis=-1, keepdims=True))
            d = jnp.sum(e, axis=-1, keepdims=True)
            o_ref[:, h * hd:(h + 1) * hd] = _nn(e.astype(bf16), vh) * (1.0 / d)

    return pl.pallas_call(
        body, grid=(T // tq,),
        in_specs=[pl.BlockSpec((tq, D), lambda i: (i, 0)), pl.BlockSpec((M, 2 * D), lambda i: (i // tps, 0))],
        out_specs=pl.BlockSpec((tq, D), lambda i: (i, 0)), out_shape=jax.ShapeDtypeStruct((T, D), f32),
        compiler_params=_params(("parallel",), 40), name="mem_attn_prompt")(q, mkv)


def _mem_attn_sample(q, cache, layer, dec_seq):
    T, D = q.shape
    _, S, M, _, H, hd = cache.shape
    scale = hd ** -0.5
    spb = 8 // dec_seq
    assert 8 % dec_seq == 0 and S % spb == 0

    def body(q_ref, kv_ref, o_ref):
        for s in range(spb):
            rows = slice(s * dec_seq, (s + 1) * dec_seq)
            for h in range(H):
                qh = (q_ref[rows, h * hd:(h + 1) * hd] * scale).astype(bf16)
                kh = kv_ref[s, :, 0, h, :].astype(bf16)
                vh = kv_ref[s, :, 1, h, :].astype(bf16)
                sc = _nt(qh, kh)
                e = jnp.exp(sc - jnp.max(sc, axis=-1, keepdims=True))
                d = jnp.sum(e, axis=-1, keepdims=True)
                o_ref[rows, h * hd:(h + 1) * hd] = _nn(e.astype(bf16), vh) * (1.0 / d)

    return pl.pallas_call(
        body, grid=(S // spb,),
        in_specs=[pl.BlockSpec((8, D), lambda i: (i, 0)),
                  pl.BlockSpec((None, spb, M, 2, H, hd), lambda i: (layer, i, 0, 0, 0, 0))],
        out_specs=pl.BlockSpec((8, D), lambda i: (i, 0)), out_shape=jax.ShapeDtypeStruct((T, D), f32),
        compiler_params=_params(("parallel",), 40), name="mem_attn_sample")(q, cache)


def _cumsum_steps(n):
    k, out = 1, []
    while k < n:
        out.append(k)
        k *= 2
    return out


def _ssd_prompt(xbc, z, small, conv_w, conv_b, dt_bias, a_log, d_skip, ssm_norm, nb):
    T, C = xbc.shape
    d_ssm = z.shape[1]
    H = d_ssm // SSM_HEAD_DIM
    P, N, L = SSM_HEAD_DIM, SSM_STATE, SSD_CHUNK
    hpg = H // SSM_GROUPS
    kw = conv_w.shape[0]
    seq = T // nb
    nc = seq // L
    assert seq % L == 0 and kw - 1 <= 8

    pad = lambda v: jnp.zeros((1, LANES), f32).at[0, :H].set(v)
    col = lambda v: jnp.broadcast_to(v[:, None], (H, LANES))

    def body(xbc_ref, z_ref, sm_ref, cw_ref, cb_ref, dtb_r, alog_r, dsk_r, dtb_c, alog_c, nrm_ref,
             y_ref, st_ref, cv_ref, xpad, hst):
        c = pl.program_id(1)

        @pl.when(c == 0)
        def _():
            xpad[0:8] = jnp.zeros((8, C), f32)
            hst[...] = jnp.zeros_like(hst)

        xpad[8:8 + L] = xbc_ref[...]
        acc = cb_ref[...] + cw_ref[0:1] * xpad[9 - kw:9 - kw + L]
        for k in range(1, kw):
            acc += cw_ref[k:k + 1] * xpad[9 - kw + k:9 - kw + k + L]
        xc = jax.nn.silu(acc)
        xpad[0:8] = xpad[L:L + 8]

        @pl.when(c == nc - 1)
        def _():
            cv_ref[...] = xpad[9 - kw:8]

        sm = sm_ref[...]
        dt_c = jax.nn.softplus(sm + dtb_r[...])
        acs_c = dt_c * (-jnp.exp(alog_r[...]))
        row = lax.broadcasted_iota(jnp.int32, (L, LANES), 0)
        for k in _cumsum_steps(L):
            acs_c = acs_c + jnp.where(row >= k, pltpu.roll(acs_c, k, 0), 0.0)
        dt_r = jax.nn.softplus(sm.T[0:H] + dtb_c[...])
        acs_r = dt_r * (-jnp.exp(alog_c[...]))
        lane = lax.broadcasted_iota(jnp.int32, (H, L), 1)
        for k in _cumsum_steps(L):
            acs_r = acs_r + jnp.where(lane >= k, pltpu.roll(acs_r, k, 1), 0.0)

        causal = lax.broadcasted_iota(jnp.int32, (L, L), 0) >= lax.broadcasted_iota(jnp.int32, (L, L), 1)
        ys = []
        for g in range(SSM_GROUPS):
            bg = xc[:, d_ssm + g * N:d_ssm + (g + 1) * N].astype(bf16)
            cg = xc[:, d_ssm + (SSM_GROUPS + g) * N:d_ssm + (SSM_GROUPS + g + 1) * N].astype(bf16)
            cb = _nt(cg, bg)
            hg = hst[g * hpg:(g + 1) * hpg].reshape(hpg * P, N)
            yoff = _nt(cg, hg.astype(bf16))
            xds, alasts = [], []
            for c4 in range(hpg):
                h = g * hpg + c4
                ai = acs_c[:, h:h + 1]
                aj = acs_r[h:h + 1, :]
                dec = jnp.exp(jnp.where(causal, ai - aj, NEG))
                xs = xc[:, h * P:(h + 1) * P]
                xdt = xs * dt_c[:, h:h + 1]
                ydiag = _nn((cb * dec).astype(bf16), xdt.astype(bf16))
                alast = acs_c[L - 1:L, h:h + 1]
                ys.append(ydiag + yoff[:, c4 * P:(c4 + 1) * P] * jnp.exp(ai) + dsk_r[:, h:h + 1] * xs)
                xds.append((xdt * jnp.exp(alast - ai)).astype(bf16))
                alasts.append(alast)
            dh = _tn(jnp.concatenate(xds, axis=1), bg)
            for c4 in range(hpg):
                h = g * hpg + c4
                hst[h] = hst[h] * jnp.exp(alasts[c4]) + dh[c4 * P:(c4 + 1) * P]

        y = jnp.concatenate(ys, axis=1) * jax.nn.silu(z_ref[...])
        y_ref[...] = _rms(y, nrm_ref[...])

        @pl.when(c == nc - 1)
        def _():
            st_ref[...] = hst[...]

    tok = lambda w: pl.BlockSpec((L, w), lambda b, c: (b * nc + c, 0))
    cst = lambda shp: pl.BlockSpec(shp, lambda b, c: tuple(0 for _ in shp))
    return pl.pallas_call(
        body, grid=(nb, nc),
        in_specs=[tok(C), tok(d_ssm), tok(LANES), cst((kw, C)), cst((1, C)), cst((1, LANES)), cst((1, LANES)),
                  cst((1, LANES)), cst((H, LANES)), cst((H, LANES)), cst((1, d_ssm))],
        out_specs=[tok(d_ssm), pl.BlockSpec((None, H, P, N), lambda b, c: (b, 0, 0, 0)),
                   pl.BlockSpec((None, kw - 1, C), lambda b, c: (b, 0, 0))],
        out_shape=[jax.ShapeDtypeStruct((T, d_ssm), f32), jax.ShapeDtypeStruct((nb, H, P, N), f32),
                   jax.ShapeDtypeStruct((nb, kw - 1, C), f32)],
        scratch_shapes=[pltpu.VMEM((L + 8, C), f32), pltpu.VMEM((H, P, N), f32)],
        compiler_params=_params(("parallel", "arbitrary"), 32), name="ssd_prompt",
    )(xbc, z, small, conv_w, conv_b.reshape(1, C), pad(dt_bias), pad(a_log), pad(d_skip), col(dt_bias), col(a_log),
      ssm_norm.reshape(1, d_ssm))


def _ssd_sample(xbc, z, small, state, conv_prev, layer, conv_w, conv_b, dt_bias, a_log, d_skip, ssm_norm, dec_seq):
    T, C = xbc.shape
    S = T // dec_seq
    d_ssm = z.shape[1]
    H = d_ssm // SSM_HEAD_DIM
    P, N = SSM_HEAD_DIM, SSM_STATE
    hpg = H // SSM_GROUPS
    kw = conv_w.shape[0]
    Lt = dec_seq
    tmaj = lambda a: a.reshape(S, Lt, a.shape[1]).transpose(1, 0, 2)

    def conv_body(x_ref, prev_ref, sm_ref, cw_ref, cb_ref, xt_ref, smt_ref, cv_ref):
        rows = [prev_ref[k] for k in range(kw - 1)] + [x_ref[t] for t in range(Lt)]
        for t in range(Lt):
            acc = cb_ref[...] + cw_ref[0:1] * rows[t]
            for k in range(1, kw):
                acc += cw_ref[k:k + 1] * rows[t + k]
            xt_ref[t] = jax.nn.silu(acc).T
            smt_ref[t] = sm_ref[t].T
        for k in range(kw - 1):
            cv_ref[k] = rows[Lt + k]

    xt, smt, conv_new = pl.pallas_call(
        conv_body, grid=(1,),
        in_specs=[pl.BlockSpec((Lt, S, C), lambda i: (0, 0, 0)), pl.BlockSpec((None, kw - 1, S, C), lambda i: (layer, 0, 0, 0)),
                  pl.BlockSpec((Lt, S, LANES), lambda i: (0, 0, 0)), pl.BlockSpec((kw, C), lambda i: (0, 0)),
                  pl.BlockSpec((1, C), lambda i: (0, 0))],
        out_specs=[pl.BlockSpec((Lt, C, S), lambda i: (0, 0, 0)), pl.BlockSpec((Lt, LANES, S), lambda i: (0, 0, 0)),
                   pl.BlockSpec((kw - 1, S, C), lambda i: (0, 0, 0))],
        out_shape=[jax.ShapeDtypeStruct((Lt, C, S), f32), jax.ShapeDtypeStruct((Lt, LANES, S), f32),
                   jax.ShapeDtypeStruct((kw - 1, S, C), f32)],
        compiler_params=_params(("arbitrary",), 32), name="ssd_sample_conv",
    )(tmaj(xbc), conv_prev, tmaj(small), conv_w, conv_b.reshape(1, C))

    def state_body(st_ref, xt_ref, smt_ref, dtb_ref, alog_ref, dsk_ref, nst_ref, y_ref, xdt_s):
        h = pl.program_id(0)
        g = h // hpg
        xoff = pl.multiple_of(h * P, P)
        boff = pl.multiple_of(d_ssm + g * N, N)
        coff = pl.multiple_of(d_ssm + (SSM_GROUPS + g) * N, N)
        a = -jnp.exp(alog_ref[pl.ds(h, 1), :])
        dsk = dsk_ref[pl.ds(h, 1), :]
        decs, bs, cs = [], [], []
        for t in range(Lt):
            dt = jax.nn.softplus(smt_ref[t, pl.ds(h, 1), :] + dtb_ref[pl.ds(h, 1), :])
            x = xt_ref[t, pl.ds(xoff, P), :]
            decs.append(jnp.exp(dt * a))
            xdt_s[t] = x * dt
            bs.append(xt_ref[t, pl.ds(boff, N), :])
            cs.append(xt_ref[t, pl.ds(coff, N), :])
            y_ref[t] = dsk * x

        def step(p, carry):
            r0 = pl.multiple_of(p * N, N)
            hp = st_ref[pl.ds(r0, N), :]
            for t in range(Lt):
                hp = hp * decs[t] + xdt_s[t, pl.ds(p, 1), :] * bs[t]
                y_ref[t, pl.ds(p, 1), :] += jnp.sum(cs[t] * hp, axis=0, keepdims=True)
            nst_ref[pl.ds(r0, N), :] = hp
            return carry

        lax.fori_loop(0, P, step, 0)

    colS = lambda v: jnp.broadcast_to(v[:, None], (H, S))
    new_state, yt = pl.pallas_call(
        state_body, grid=(H,),
        in_specs=[pl.BlockSpec((None, None, P * N, S), lambda h: (layer, h, 0, 0)),
                  pl.BlockSpec((Lt, C, S), lambda h: (0, 0, 0)), pl.BlockSpec((Lt, LANES, S), lambda h: (0, 0, 0)),
                  pl.BlockSpec((H, S), lambda h: (0, 0)), pl.BlockSpec((H, S), lambda h: (0, 0)),
                  pl.BlockSpec((H, S), lambda h: (0, 0))],
        out_specs=[pl.BlockSpec((None, P * N, S), lambda h: (h, 0, 0)), pl.BlockSpec((Lt, P, S), lambda h: (0, h, 0))],
        out_shape=[jax.ShapeDtypeStruct((H, P * N, S), f32), jax.ShapeDtypeStruct((Lt, d_ssm, S), f32)],
        scratch_shapes=[pltpu.VMEM((Lt, P, S), f32)],
        compiler_params=_params(("parallel",), 32), name="ssd_sample_state",
    )(state, xt, smt, colS(dt_bias), colS(a_log), colS(d_skip))

    def out_body(yt_ref, z_ref, nrm_ref, o_ref):
        for t in range(Lt):
            y = yt_ref[t].T * jax.nn.silu(z_ref[t])
            o_ref[t] = _rms(y, nrm_ref[...])

    y = pl.pallas_call(
        out_body, grid=(1,),
        in_specs=[pl.BlockSpec((Lt, d_ssm, S), lambda i: (0, 0, 0)), pl.BlockSpec((Lt, S, d_ssm), lambda i: (0, 0, 0)),
                  pl.BlockSpec((1, d_ssm), lambda i: (0, 0))],
        out_specs=pl.BlockSpec((Lt, S, d_ssm), lambda i: (0, 0, 0)),
        out_shape=jax.ShapeDtypeStruct((Lt, S, d_ssm), f32),
        compiler_params=_params(("arbitrary",), 32), name="ssd_sample_out",
    )(yt, tmaj(z), ssm_norm.reshape(1, d_ssm))
    return y.transpose(1, 0, 2).reshape(T, d_ssm), new_state, conv_new


def _slot_block(slot):
    return jnp.where(slot < LANES // 2, 2 * slot, 2 * (slot - LANES // 2) + 1)


def _pool_consts(cmp_wk, cmp_wv, rows):
    r = jnp.arange(rows)
    tile = lambda w: jnp.tile(w.T[:, r % CMP_BLOCK], (N_KV, 1))
    wt = jnp.concatenate([tile(cmp_wk), tile(cmp_wv)], axis=0)
    blk = r // CMP_BLOCK
    slot = jnp.where(blk % 2 == 0, blk // 2, LANES // 2 + blk // 2)
    seg = (slot[:, None] == jnp.arange(LANES)[None, :]).astype(bf16)
    return wt, seg


def _expand_const(n_keys):
    blk = jnp.arange(n_keys) // SEL_BLOCK
    return (jnp.arange(LANES)[:, None] == blk[None, :]).astype(bf16)


def _cmp_pool_prompt(kvt, wt, seg):
    nb, _, L = kvt.shape
    F = 2 * N_KV * HEAD_DIM

    def body(kv_ref, wt_ref, seg_ref, segt_ref, pool_ref, poolt_ref):
        hi, lo = _split2(kv_ref[...] * wt_ref[...])
        poolt_ref[...] = _nn(hi, seg_ref[...]) + _nn(lo, seg_ref[...])
        pool_ref[...] = _nt(segt_ref[...], hi) + _nt(segt_ref[...], lo)

    return pl.pallas_call(
        body, grid=(nb,),
        in_specs=[pl.BlockSpec((None, F, L), lambda b: (b, 0, 0)), pl.BlockSpec((F, L), lambda b: (0, 0)),
                  pl.BlockSpec((L, LANES), lambda b: (0, 0)), pl.BlockSpec((LANES, L), lambda b: (0, 0))],
        out_specs=[pl.BlockSpec((None, LANES, F), lambda b: (b, 0, 0)), pl.BlockSpec((None, F, LANES), lambda b: (b, 0, 0))],
        out_shape=[jax.ShapeDtypeStruct((nb, LANES, F), f32), jax.ShapeDtypeStruct((nb, F, LANES), f32)],
        compiler_params=_params(("parallel",), 32), name="cmp_pool_prompt")(kvt, wt, seg, seg.T)


def _select_blocks(score, valid, blk, n_blocks, axis):
    cnt = jnp.zeros(score.shape, f32)
    for i in range(n_blocks):
        si = score[i:i + 1, :] if axis == 0 else score[:, i:i + 1]
        beat = (si > score) | ((si == score) & (blk > i))
        cnt += jnp.where(beat, 1.0, 0.0)
    return jnp.where((cnt < TOP_N) & valid, 1.0, 0.0)


def _nsa_prompt(q, small, kvt, wint, pool, poolt, expand, gate_b, goff, tq=128):
    T, dq = q.shape
    nb, _, L = kvt.shape
    hd = HEAD_DIM
    H = dq // hd
    hpg = H // N_KV
    F = 2 * N_KV * hd
    nq = L // tq
    scale = hd ** -0.5
    n_sel = L // SEL_BLOCK
    span = tq + WINDOW
    assert L % SEL_TK == 0 and L % tq == 0 and tq % SEL_BLOCK == 0 and n_sel <= LANES // 2 and WINDOW % LANES == 0

    def body(q_ref, sm_ref, kv_ref, win_ref, pool_ref, poolt_ref, e_ref, gb_ref, o_ref, kbf, wbf, bias_ref):
        qi = pl.program_id(1)

        @pl.when(qi == 0)
        def _():
            kbf[...] = kv_ref[...].astype(bf16)
            wbf[:, 0:WINDOW] = jnp.zeros((F, WINDOW), bf16)
            wbf[:, WINDOW:] = win_ref[...].astype(bf16)

        q0 = pl.multiple_of(qi * tq, tq)
        qs = q_ref[...] * scale
        gates = jax.nn.sigmoid(sm_ref[...] + gb_ref[...])
        qpos_c = q0 + lax.broadcasted_iota(jnp.int32, (tq, 1), 0)
        qpos_r = q0 + lax.broadcasted_iota(jnp.int32, (1, tq), 1)
        qpos_c4 = jnp.concatenate([qpos_c] * hpg, axis=0)
        qpos_r4 = jnp.concatenate([qpos_r] * hpg, axis=1)
        slot_r = _slot_block(lax.broadcasted_iota(jnp.int32, (1, LANES), 1))
        slot_c = _slot_block(lax.broadcasted_iota(jnp.int32, (LANES, 1), 0))
        outs = []
        for g in range(N_KV):
            qg = jnp.concatenate([qs[:, (g * hpg + c) * hd:(g * hpg + c + 1) * hd] for c in range(hpg)], axis=0).astype(bf16)
            kct = poolt_ref[g * hd:(g + 1) * hd, :].astype(bf16)
            kc = pool_ref[:, g * hd:(g + 1) * hd].astype(bf16)
            vc = pool_ref[:, (N_KV + g) * hd:(N_KV + g + 1) * hd].astype(bf16)
            p = _masked_softmax(_nn(qg, kct), (slot_r * CMP_BLOCK + CMP_BLOCK - 1) <= qpos_c4)
            o_cmp = _nn(p.astype(bf16), vc)
            pt = _masked_softmax(_nt(kc, qg), (slot_c * CMP_BLOCK + CMP_BLOCK - 1) <= qpos_r4, axis=0)
            impt = pt[:, 0:tq]
            for c in range(1, hpg):
                impt = impt + pt[:, c * tq:(c + 1) * tq]
            imp = impt[0:n_sel] + impt[LANES // 2:LANES // 2 + n_sel]
            blk = lax.broadcasted_iota(jnp.int32, (n_sel, 1), 0)
            cur = qpos_r // SEL_BLOCK
            valid = blk <= cur
            forced = (blk == 0) | (blk == cur) | (blk == cur - 1)
            score = jnp.where(valid, jnp.where(forced, FORCE_SCORE, imp), NEG)
            selt = _select_blocks(score, valid, blk, n_sel, axis=0)
            sel = jnp.concatenate([selt, jnp.zeros((LANES - n_sel, tq), f32)], axis=0).T
            expd = _nn(sel.astype(bf16), e_ref[...])
            kpos = lax.broadcasted_iota(jnp.int32, (1, L), 1)
            bias_ref[...] = jnp.where((expd > 0.5) & (kpos <= qpos_c), 0.0, NEG)

            def sel_step(j, carry):
                m, l, acc = carry
                k0 = pl.multiple_of(j * SEL_TK, SEL_TK)
                kt = kbf[g * hd:(g + 1) * hd, pl.ds(k0, SEL_TK)]
                vt = kbf[(N_KV + g) * hd:(N_KV + g + 1) * hd, pl.ds(k0, SEL_TK)]
                s = _nn(qg, kt).reshape(hpg, tq, SEL_TK) + bias_ref[:, pl.ds(k0, SEL_TK)][None]
                s = s.reshape(hpg * tq, SEL_TK)
                m_new = jnp.maximum(m, jnp.max(s, axis=-1, keepdims=True))
                alpha = jnp.exp(m - m_new)
                pj = jnp.exp(s - m_new)
                l = alpha * l + jnp.sum(pj, axis=-1, keepdims=True)
                acc = alpha * acc + _nt(pj.astype(bf16), vt)
                return m_new, l, acc

            init = (jnp.full((hpg * tq, 1), NEG, f32), jnp.zeros((hpg * tq, 1), f32), jnp.zeros((hpg * tq, hd), f32))
            _, l_sel, acc_sel = lax.fori_loop(0, (q0 + tq + SEL_TK - 1) // SEL_TK, sel_step, init)
            o_sel = acc_sel * (1.0 / l_sel)
            kw_ = wbf[g * hd:(g + 1) * hd, pl.ds(q0, span)]
            vw_ = wbf[(N_KV + g) * hd:(N_KV + g + 1) * hd, pl.ds(q0, span)]
            col = lax.broadcasted_iota(jnp.int32, (1, span), 1)
            dist = (lax.broadcasted_iota(jnp.int32, (tq, 1), 0) + WINDOW) - col
            wmask = (dist >= 0) & (dist < WINDOW) & (col + q0 >= WINDOW)
            wmask4 = jnp.concatenate([wmask] * hpg, axis=0)
            pw = _masked_softmax(_nn(qg, kw_), wmask4)
            o_win = _nt(pw.astype(bf16), vw_)
            for c in range(hpg):
                h = g * hpg + c
                rows = slice(c * tq, (c + 1) * tq)
                gc, gs, gw = (gates[:, goff + k * H + h:goff + k * H + h + 1] for k in range(3))
                outs.append(gc * o_cmp[rows] + gs * o_sel[rows] + gw * o_win[rows])
        o_ref[...] = jnp.concatenate(outs, axis=1)

    tok = lambda w: pl.BlockSpec((tq, w), lambda b, i: (b * nq + i, 0))
    return pl.pallas_call(
        body, grid=(nb, nq),
        in_specs=[tok(dq), tok(LANES), pl.BlockSpec((None, F, L), lambda b, i: (b, 1, 0)),
                  pl.BlockSpec((None, F, L), lambda b, i: (b, 0, 0)),
                  pl.BlockSpec((None, LANES, F), lambda b, i: (b, 0, 0)), pl.BlockSpec((None, F, LANES), lambda b, i: (b, 0, 0)),
                  pl.BlockSpec((LANES, L), lambda b, i: (0, 0)), pl.BlockSpec((1, LANES), lambda b, i: (0, 0))],
        out_specs=tok(dq), out_shape=jax.ShapeDtypeStruct((T, dq), f32),
        scratch_shapes=[pltpu.VMEM((F, L), bf16), pltpu.VMEM((F, L + WINDOW), bf16), pltpu.VMEM((tq, L), f32)],
        compiler_params=_params(("parallel", "arbitrary"), 48), name="nsa_prompt",
    )(q, small, kvt, wint, pool, poolt, expand, gate_b)


def _nsa_sample(q, kv_new, win_new, small, cache_t, win_t, page_table, layer, wt_page, seg, segt, expand, gate_b, goff, dec_seq):
    T, dq = q.shape
    S, n_pages = page_table.shape
    page = cache_t.shape[-1]
    keep = win_t.shape[-1]
    past = n_pages * page
    hd = HEAD_DIM
    H = dq // hd
    hpg = H // N_KV
    F = 2 * N_KV * hd
    Lt = dec_seq
    spb = 8 // Lt
    nr = hpg * Lt
    scale = hd ** -0.5
    n_sel = past // SEL_BLOCK + 1
    new_blk = past // SEL_BLOCK
    k_start = past - keep
    assert 8 % Lt == 0 and S % spb == 0 and past % SEL_BLOCK == 0 and Lt <= SEL_BLOCK and n_sel <= LANES // 2
    assert page == LANES and keep % LANES == 0 and keep <= WINDOW and Lt <= LANES

    def body(pt_ref, q_ref, kvn_ref, wn_ref, sm_ref, *refs):
        pages = refs[:n_pages]
        win_ref, wt_ref, seg_ref, segt_ref, e_ref, gb_ref, shift_ref, o_ref, wout_ref = refs[n_pages:]
        j = pl.program_id(1)
        pick = lambda r: jnp.where(j == 0, r[0:Lt], r[Lt:2 * Lt]) if spb == 2 else r[0:Lt]
        qs = pick(q_ref[...]) * scale
        kvn = pick(kvn_ref[...])
        wn = pick(wn_ref[...])
        gates = jax.nn.sigmoid(pick(sm_ref[...]) + gb_ref[...])

        poolt = jnp.zeros((F, LANES), f32)
        pool = jnp.zeros((LANES, F), f32)
        for p in range(n_pages):
            hi, lo = _split2(pages[p][0:F, :] * wt_ref[...])
            poolt += _nn(hi, seg_ref[p]) + _nn(lo, seg_ref[p])
            pool += _nt(segt_ref[p], hi) + _nt(segt_ref[p], lo)

        tpos = lax.broadcasted_iota(jnp.int32, (Lt, 1), 0)
        qpos = past + tpos
        qpos_n = jnp.concatenate([qpos] * hpg, axis=0)
        tpos_n = jnp.concatenate([tpos] * hpg, axis=0)
        lane = lax.broadcasted_iota(jnp.int32, (1, LANES), 1)
        slot_r = _slot_block(lane)
        outs = []
        for g in range(N_KV):
            qg32 = jnp.concatenate([qs[:, (g * hpg + c) * hd:(g * hpg + c + 1) * hd] for c in range(hpg)], axis=0)
            qg = qg32.astype(bf16)
            p = _masked_softmax(_nn(qg, poolt[g * hd:(g + 1) * hd].astype(bf16)), (slot_r * CMP_BLOCK + CMP_BLOCK - 1) <= qpos_n)
            o_cmp = _nn(p.astype(bf16), pool[:, (N_KV + g) * hd:(N_KV + g + 1) * hd].astype(bf16))
            imp = p[0:Lt]
            for c in range(1, hpg):
                imp = imp + p[c * Lt:(c + 1) * Lt]
            imp = imp + pltpu.roll(imp, LANES // 2, 1)
            cur = qpos // SEL_BLOCK
            valid = (lane <= cur) & (lane < n_sel)
            forced = (lane == 0) | (lane == cur) | (lane == cur - 1)
            score = jnp.where(valid, jnp.where(forced, FORCE_SCORE, imp), NEG)
            sel = _select_blocks(score, valid, lane, n_sel, axis=1)
            expd = _nn(sel.astype(bf16), e_ref[...])
            kpos = lax.broadcasted_iota(jnp.int32, (1, past), 1)
            bias = jnp.where((expd > 0.5) & (kpos <= qpos), 0.0, NEG)
            bias_n = jnp.concatenate([bias] * hpg, axis=0)
            sel_new = jnp.concatenate([sel[:, new_blk:new_blk + 1]] * hpg, axis=0) > 0.5

            def new_scores(k_rows, ok):
                cols = []
                for t in range(Lt):
                    s_t = jnp.sum(qg32 * k_rows[t:t + 1, :], axis=-1, keepdims=True)
                    cols.append(jnp.where(ok & (tpos_n >= t), s_t, NEG))
                return cols

            def attend(s_past, s_new, v_past_fn, v_new):
                m = jnp.max(s_past, axis=-1, keepdims=True)
                for s_t in s_new:
                    m = jnp.maximum(m, s_t)
                e_past = jnp.exp(s_past - m)
                den = jnp.sum(e_past, axis=-1, keepdims=True)
                acc = v_past_fn(e_past.astype(bf16))
                for t, s_t in enumerate(s_new):
                    e_t = jnp.exp(s_t - m)
                    den = den + e_t
                    acc = acc + e_t * v_new[t:t + 1, :]
                return acc * (1.0 / den)

            s_past = jnp.concatenate([_nn(qg, pages[p][(2 * N_KV + g) * hd:(2 * N_KV + g + 1) * hd, :].astype(bf16))
                                      for p in range(n_pages)], axis=1) + bias_n

            def v_sel(e):
                acc = _nt(e[:, 0:page], pages[0][(3 * N_KV + g) * hd:(3 * N_KV + g + 1) * hd, :].astype(bf16))
                for p in range(1, n_pages):
                    acc += _nt(e[:, p * page:(p + 1) * page], pages[p][(3 * N_KV + g) * hd:(3 * N_KV + g + 1) * hd, :].astype(bf16))
                return acc

            o_sel = attend(s_past, new_scores(kvn[:, (2 * N_KV + g) * hd:(2 * N_KV + g + 1) * hd], sel_new),
                           v_sel, kvn[:, (3 * N_KV + g) * hd:(3 * N_KV + g + 1) * hd])
            kpw = k_start + lax.broadcasted_iota(jnp.int32, (1, keep), 1)
            dist = qpos_n - kpw
            s_w = jnp.where((dist >= 0) & (dist < WINDOW), _nn(qg, win_ref[g * hd:(g + 1) * hd, :].astype(bf16)), NEG)
            always = jnp.full((nr, 1), True)
            o_win = attend(s_w, new_scores(wn[:, g * hd:(g + 1) * hd], always),
                           lambda e: _nt(e, win_ref[(N_KV + g) * hd:(N_KV + g + 1) * hd, :].astype(bf16)),
                           wn[:, (N_KV + g) * hd:(N_KV + g + 1) * hd])
            for c in range(hpg):
                h = g * hpg + c
                rows = slice(c * Lt, (c + 1) * Lt)
                gc, gs, gw = (gates[:, goff + k * H + h:goff + k * H + h + 1] for k in range(3))
                outs.append(gc * o_cmp[rows] + gs * o_sel[rows] + gw * o_win[rows])
        o = jnp.concatenate(outs, axis=1)
        if spb == 2:
            @pl.when(j == 0)
            def _():
                o_ref[0:Lt] = o

            @pl.when(j == 1)
            def _():
                o_ref[Lt:2 * Lt] = o
        else:
            o_ref[0:Lt] = o

        w_old = win_ref[...]
        rolled = pltpu.roll(w_old, keep - Lt, 1)
        hi, mid, lo = _split3(wn)
        new_t = _tn(hi, shift_ref[...]) + _tn(mid, shift_ref[...]) + _tn(lo, shift_ref[...])
        last = jnp.where(lane >= LANES - Lt, new_t, rolled[:, keep - LANES:keep])
        if keep > LANES:
            wout_ref[:, 0:keep - LANES] = rolled[:, 0:keep - LANES]
        wout_ref[:, keep - LANES:keep] = last

    shift = (jnp.arange(Lt)[:, None] + (LANES - Lt) == jnp.arange(LANES)[None, :]).astype(bf16)
    tok = lambda w: pl.BlockSpec((8, w), lambda i, j, pt: (i, 0))
    cst = lambda shp: pl.BlockSpec(shp, lambda i, j, pt: tuple(0 for _ in shp))
    page_spec = lambda p: pl.BlockSpec((None, None, 2 * F, page), lambda i, j, pt: (layer, pt[i * spb + j, p], 0, 0))
    wd = win_t.shape[2]
    grid_spec = pltpu.PrefetchScalarGridSpec(
        num_scalar_prefetch=1, grid=(S // spb, spb),
        in_specs=[tok(dq), tok(2 * F), tok(wd), tok(LANES)] + [page_spec(p) for p in range(n_pages)] + [
            pl.BlockSpec((None, None, wd, keep), lambda i, j, pt: (layer, i * spb + j, 0, 0)),
            cst((F, page)), cst((n_pages, page, LANES)), cst((n_pages, LANES, page)), cst((LANES, past)), cst((1, LANES)),
            cst((Lt, LANES))],
        out_specs=[tok(dq), pl.BlockSpec((None, wd, keep), lambda i, j, pt: (i * spb + j, 0, 0))])
    return pl.pallas_call(
        body, grid_spec=grid_spec,
        out_shape=[jax.ShapeDtypeStruct((T, dq), f32), jax.ShapeDtypeStruct((S, wd, keep), f32)],
        compiler_params=_params(("parallel", "arbitrary"), 48), name="nsa_sample",
    )(page_table, q, kv_new, win_new, small, *([cache_t] * n_pages), win_t, wt_page, seg, segt, expand, gate_b, shift)


def kernel(x_prompt, x_sample, mem_prompt, cache_nsa_kv, cache_mem_kv, state_nsa_win, state_ssm, state_conv, page_table, norm_mix_pre, w_in, conv_w, conv_b, dt_bias, a_log, d_skip, ssm_norm, cmp_wk, cmp_wv, gate_b, w_out, norm_mix_post, norm_mem_src, w_mem_q, w_mem_k, w_mem_v, w_mem_o, norm_mem_pre, norm_mem_post, norm_ffn_pre, w_up, w_down, norm_ffn_post):
    B, L, D = x_prompt.shape
    S, Lt, _ = x_sample.shape
    depth = w_in.shape[0]
    M = mem_prompt.shape[1]
    d_ssm = ssm_norm.shape[1]
    C = conv_w.shape[2]
    H = dt_bias.shape[1]
    P, N = state_ssm.shape[3], state_ssm.shape[4]
    d_att = w_out.shape[1] - d_ssm
    Ha = d_att // HEAD_DIM
    n_pool, page = cache_nsa_kv.shape[1], cache_nsa_kv.shape[2]
    keep = state_nsa_win.shape[2]
    kvw = 4 * N_KV * HEAD_DIM
    winw = 2 * N_KV * HEAD_DIM
    past = page_table.shape[1] * page
    assert H + 3 * Ha <= LANES

    cache_t = cache_nsa_kv.transpose(0, 1, 3, 4, 5, 2).reshape(depth, n_pool, kvw, page)
    win_t = state_nsa_win.transpose(0, 1, 3, 4, 5, 2).reshape(depth, S, winw, keep)
    ssm_t = state_ssm.transpose(0, 2, 3, 4, 1).reshape(depth, H, P * N, S)
    conv_t = state_conv.transpose(0, 2, 1, 3)

    o1 = d_ssm
    o2 = o1 + C
    o3 = o2 + H
    o4 = o3 + d_att
    o5 = o4 + kvw
    o6 = o5 + winw
    n_gate = 3 * Ha

    expand_p = _expand_const(L)
    expand_s = _expand_const(past)
    xp = x_prompt.reshape(B * L, D)
    xs = x_sample.reshape(S * Lt, D)
    mem = mem_prompt.reshape(B * M, D)
    kv_p, kv_s, win_p, win_s, ssm_p, ssm_s, conv_p, conv_s, mkv_out = [], [], [], [], [], [], [], [], []
    for l in range(depth):
        wl = w_in[l].astype(bf16)
        w_z, w_xbc, w_q = wl[:, :o1], wl[:, o1:o2], wl[:, o3:o4]
        w_kv, w_win = wl[:, o4:o5], wl[:, o5:o6]
        w_small = jnp.concatenate([wl[:, o2:o3], wl[:, o6:o6 + n_gate], jnp.zeros((D, LANES - H - n_gate), bf16)], axis=1)
        gb = jnp.zeros((1, LANES), f32).at[0, H:H + n_gate].set(gate_b[l])
        wo = w_out[l].astype(bf16)
        wt_full, seg_full = _pool_consts(cmp_wk[l], cmp_wv[l], L)
        wt_page, seg_page = _pool_consts(cmp_wk[l], cmp_wv[l], past)
        seg_pages = seg_page.reshape(past // page, page, LANES)

        mkv, = _norm_proj(mem, norm_mem_src[l], [jnp.concatenate([w_mem_k[l], w_mem_v[l]], axis=1).astype(bf16)])
        mkv_out.append(mkv.reshape(B, M, 2, MEM_HEADS, D // MEM_HEADS))

        z, xbc, q, small, kvt, wint = _norm_proj(xp, norm_mix_pre[l], [w_z, w_xbc, w_q, w_small], [w_kv.T, w_win.T], seq=L)
        y_ssd, st_new, cv_new = _ssd_prompt(xbc, z, small, conv_w[l], conv_b[l], dt_bias[l], a_log[l], d_skip[l], ssm_norm[l], B)
        pool, poolt = _cmp_pool_prompt(kvt, wt_full, seg_full)
        o_att = _nsa_prompt(q, small, kvt, wint, pool, poolt, expand_p, gb, H)
        xp = _linear_post(xp, [y_ssd, o_att], [wo[:d_ssm], wo[d_ssm:]], norm_mix_post[l])
        kv_p.append(kvt)
        win_p.append(wint[:, :, L - min(WINDOW, L):])
        ssm_p.append(st_new)
        conv_p.append(cv_new)

        z, xbc, q, small, kvr, winr = _norm_proj(xs, norm_mix_pre[l], [w_z, w_xbc, w_q, w_small, w_kv, w_win])
        y_ssd, st_new, cv_new = _ssd_sample(xbc, z, small, ssm_t, conv_t, l, conv_w[l], conv_b[l], dt_bias[l], a_log[l],
                                            d_skip[l], ssm_norm[l], Lt)
        o_att, wnew = _nsa_sample(q, kvr, winr, small, cache_t, win_t, page_table, l, wt_page[:, :page], seg_pages,
                                  seg_pages.transpose(0, 2, 1), expand_s, gb, H, Lt)
        xs = _linear_post(xs, [y_ssd, o_att], [wo[:d_ssm], wo[d_ssm:]], norm_mix_post[l])
        kv_s.append(kvr)
        win_s.append(wnew)
        ssm_s.append(st_new)
        conv_s.append(cv_new)

        wq, wmo = w_mem_q[l].astype(bf16), w_mem_o[l].astype(bf16)
        qm, = _norm_proj(xp, norm_mem_pre[l], [wq])
        xp = _linear_post(xp, [_mem_attn_prompt(qm, mkv, L)], [wmo], norm_mem_post[l])
        qm, = _norm_proj(xs, norm_mem_pre[l], [wq])
        xs = _linear_post(xs, [_mem_attn_sample(qm, cache_mem_kv, l, Lt)], [wmo], norm_mem_post[l])

        wu, wd = w_up[l].astype(bf16), w_down[l].astype(bf16)
        xp = _ffn(xp, norm_ffn_pre[l], wu, wd, norm_ffn_post[l], tm=1024)
        xs = _ffn(xs, norm_ffn_pre[l], wu, wd, norm_ffn_post[l], tm=512)

    kv5 = (4, N_KV, HEAD_DIM)
    w5 = (2, N_KV, HEAD_DIM)
    return (xp.reshape(B, L, D), xs.reshape(S, Lt, D),
            jnp.stack(kv_p).reshape(depth, B, *kv5, L).transpose(0, 1, 5, 2, 3, 4),
            jnp.stack(kv_s).reshape(depth, S, Lt, *kv5),
            jnp.stack(win_p).reshape(depth, B, *w5, min(WINDOW, L)).transpose(0, 1, 5, 2, 3, 4),
            jnp.stack(win_s).reshape(depth, S, *w5, keep).transpose(0, 1, 5, 2, 3, 4),
            jnp.stack(ssm_p), jnp.stack(ssm_s).reshape(depth, H, P, N, S).transpose(0, 4, 1, 2, 3),
            jnp.stack(conv_p), jnp.stack(conv_s).transpose(0, 2, 1, 3),
            jnp.stack(mkv_out))
```

```python
import functools

import jax
import jax.numpy as jnp
from jax import lax
from jax.experimental import pallas as pl
from jax.experimental.pallas import tpu as pltpu

f32 = jnp.float32
bf16 = jnp.bfloat16
NEG = float("-inf")

SSM_HEAD_DIM = 64
SSM_GROUPS = 2
SSM_STATE = 64
SSD_CHUNK = 128
HEAD_DIM = 64
N_KV = 2
CMP_BLOCK = 32
SEL_BLOCK = 64
TOP_N = 16
WINDOW = 512
FORCE_SCORE = 1.0e4
MEM_HEADS = 4
EPS = 1e-6

LANES = 128
V7X_VMEM_BYTES = 64 * 1024 * 1024
SEL_TK = 512


def _params(sem, vmem_mb):
    assert vmem_mb * 1024 * 1024 < V7X_VMEM_BYTES
    return pltpu.CompilerParams(dimension_semantics=sem, vmem_limit_bytes=vmem_mb * 1024 * 1024)


def _nn(a, b):
    return jnp.dot(a, b, preferred_element_type=f32)


def _nt(a, b):
    return lax.dot_general(a, b, (((1,), (1,)), ((), ())), preferred_element_type=f32)


def _tn(a, b):
    return lax.dot_general(a, b, (((0,), (0,)), ((), ())), preferred_element_type=f32)


def _rms(x, g):
    return x * lax.rsqrt(jnp.mean(x * x, axis=-1, keepdims=True) + EPS) * g


def _masked_exp(s, mask, axis=-1):
    s = jnp.where(mask, s, NEG)
    m = jnp.max(s, axis=axis, keepdims=True)
    m = jnp.where(jnp.isfinite(m), m, 0.0)
    e = jnp.exp(s - m)
    d = jnp.sum(e, axis=axis, keepdims=True)
    return e, 1.0 / jnp.where(d > 0, d, 1.0)


def _masked_softmax(s, mask, axis=-1):
    e, r = _masked_exp(s, mask, axis)
    return e * r


def _split2(x):
    hi = x.astype(bf16)
    lo = (x - hi.astype(f32)).astype(bf16)
    return hi, lo


def _split3(x):
    hi = x.astype(bf16)
    r = x - hi.astype(f32)
    mid = r.astype(bf16)
    lo = (r - mid.astype(f32)).astype(bf16)
    return hi, mid, lo


def _norm_proj(x, g, ws, wts=(), seq=None, tm=512):
    T, D = x.shape
    tm = min(tm, T)
    assert T % tm == 0
    ws, wts = list(ws), list(wts)
    n_row = len(ws)
    tps = (seq // tm) if wts else 1

    def body(x_ref, g_ref, *refs):
        w_refs, o_refs = refs[:len(ws) + len(wts)], refs[len(ws) + len(wts):]
        xn = _rms(x_ref[...], g_ref[...]).astype(bf16)
        for k in range(n_row):
            o_refs[k][...] = _nn(xn, w_refs[k][...])
        for k in range(n_row, len(w_refs)):
            o_refs[k][...] = _nt(w_refs[k][...], xn)

    in_specs = [pl.BlockSpec((tm, D), lambda i: (i, 0)), pl.BlockSpec((1, D), lambda i: (0, 0))]
    in_specs += [pl.BlockSpec(w.shape, lambda i: (0, 0)) for w in ws + wts]
    out_specs = [pl.BlockSpec((tm, w.shape[1]), lambda i: (i, 0)) for w in ws]
    out_specs += [pl.BlockSpec((None, w.shape[0], tm), lambda i: (i // tps, 0, i % tps)) for w in wts]
    out_shape = [jax.ShapeDtypeStruct((T, w.shape[1]), f32) for w in ws]
    out_shape += [jax.ShapeDtypeStruct((T // seq, w.shape[0], seq), f32) for w in wts]
    return pl.pallas_call(body, grid=(T // tm,), in_specs=in_specs, out_specs=out_specs, out_shape=out_shape,
                          compiler_params=_params(("parallel",), 48), name="norm_proj")(x, g.reshape(1, D), *ws, *wts)


def _linear_post(res, acts, ws, g, tm=512):
    T, D = res.shape
    tm = min(tm, T)
    n = len(acts)

    def body(*refs):
        a_refs, w_refs = refs[:n], refs[n:2 * n]
        g_ref, res_ref, o_ref = refs[2 * n:]
        acc = _nn(a_refs[0][...].astype(bf16), w_refs[0][...])
        for k in range(1, n):
            acc += _nn(a_refs[k][...].astype(bf16), w_refs[k][...])
        o_ref[...] = res_ref[...] + _rms(acc, g_ref[...])

    in_specs = [pl.BlockSpec((tm, a.shape[1]), lambda i: (i, 0)) for a in acts]
    in_specs += [pl.BlockSpec(w.shape, lambda i: (0, 0)) for w in ws]
    in_specs += [pl.BlockSpec((1, D), lambda i: (0, 0)), pl.BlockSpec((tm, D), lambda i: (i, 0))]
    return pl.pallas_call(body, grid=(T // tm,), in_specs=in_specs, out_specs=pl.BlockSpec((tm, D), lambda i: (i, 0)),
                          out_shape=jax.ShapeDtypeStruct((T, D), f32),
                          compiler_params=_params(("parallel",), 40), name="linear_post")(*acts, *ws, g.reshape(1, D), res)


def _ffn(x, g1, wu, wd, g2, tm, tc=512):
    T, D = x.shape
    F = wu.shape[1]
    tm = min(tm, T)
    nj = F // tc

    def body(x_ref, g1_ref, wu_ref, wd_ref, g2_ref, o_ref, xn_s, acc_s):
        j = pl.program_id(1)

        @pl.when(j == 0)
        def _():
            xn_s[...] = _rms(x_ref[...], g1_ref[...]).astype(bf16)
            acc_s[...] = jnp.zeros_like(acc_s)

        h = _nn(xn_s[...], wu_ref[...])
        h = jnp.square(jnp.maximum(h, 0.0))
        acc_s[...] += _nn(h.astype(bf16), wd_ref[...])

        @pl.when(j == nj - 1)
        def _():
            o_ref[...] = x_ref[...] + _rms(acc_s[...], g2_ref[...])

    return pl.pallas_call(
        body, grid=(T // tm, nj),
        in_specs=[pl.BlockSpec((tm, D), lambda i, j: (i, 0)), pl.BlockSpec((1, D), lambda i, j: (0, 0)),
                  pl.BlockSpec((D, tc), lambda i, j: (0, j)), pl.BlockSpec((tc, D), lambda i, j: (j, 0)),
                  pl.BlockSpec((1, D), lambda i, j: (0, 0))],
        out_specs=pl.BlockSpec((tm, D), lambda i, j: (i, 0)),
        out_shape=jax.ShapeDtypeStruct((T, D), f32),
        scratch_shapes=[pltpu.VMEM((tm, D), bf16), pltpu.VMEM((tm, D), f32)],
        compiler_params=_params(("parallel", "arbitrary"), 48), name="ffn")(x, g1.reshape(1, D), wu, wd, g2.reshape(1, D))


def _mem_attn_prompt(q, mkv, seq, tq=512):
    T, D = q.shape
    M = mkv.shape[0] // (T // seq)
    hd = D // MEM_HEADS
    scale = hd ** -0.5
    tps = seq // tq

    def body(q_ref, kv_ref, o_ref):
        for h in range(MEM_HEADS):
            qh = (q_ref[:, h * hd:(h + 1) * hd] * scale).astype(bf16)
            kh = kv_ref[:, h * hd:(h + 1) * hd].astype(bf16)
            vh = kv_ref[:, D + h * hd:D + (h + 1) * hd].astype(bf16)
            s = _nt(qh, kh)
            e = jnp.exp(s - jnp.max(s, axis=-1, keepdims=True))
            d = jnp.sum(e, axis=-1, keepdims=True)
            o_ref[:, h * hd:(h + 1) * hd] = _nn(e.astype(bf16), vh) * (1.0 / d)

    return pl.pallas_call(
        body, grid=(T // tq,),
        in_specs=[pl.BlockSpec((tq, D), lambda i: (i, 0)), pl.BlockSpec((M, 2 * D), lambda i: (i // tps, 0))],
        out_specs=pl.BlockSpec((tq, D), lambda i: (i, 0)), out_shape=jax.ShapeDtypeStruct((T, D), f32),
        compiler_params=_params(("parallel",), 40), name="mem_attn_prompt")(q, mkv)


def _mem_attn_sample(q, cache, layer, dec_seq):
    T, D = q.shape
    _, S, M, _, H, hd = cache.shape
    scale = hd ** -0.5
    spb = 8 // dec_seq
    assert 8 % dec_seq == 0 and S % spb == 0
    ncp = spb * 2 * H

    def body(q_ref, c_ref, o_ref, buf, sem):
        i = pl.program_id(0)
        n = pl.num_programs(0)

        def copies(step, slot):
            return [pltpu.make_async_copy(c_ref.at[layer, step * spb + s, :, kv, h, :], buf.at[slot, (s * 2 + kv) * H + h],
                                          sem.at[slot, (s * 2 + kv) * H + h])
                    for s in range(spb) for kv in range(2) for h in range(H)]

        @pl.when(i == 0)
        def _():
            for cp in copies(0, 0):
                cp.start()

        @pl.when(i + 1 < n)
        def _():
            for cp in copies(i + 1, (i + 1) % 2):
                cp.start()

        slot = i % 2
        for cp in copies(i, slot):
            cp.wait()
        for s in range(spb):
            rows = slice(s * dec_seq, (s + 1) * dec_seq)
            for h in range(H):
                qh = (q_ref[rows, h * hd:(h + 1) * hd] * scale).astype(bf16)
                kh = buf[slot, (s * 2) * H + h].astype(bf16)
                vh = buf[slot, (s * 2 + 1) * H + h].astype(bf16)
                sc = _nt(qh, kh)
                e = jnp.exp(sc - jnp.max(sc, axis=-1, keepdims=True))
                d = jnp.sum(e, axis=-1, keepdims=True)
                o_ref[rows, h * hd:(h + 1) * hd] = _nn(e.astype(bf16), vh) * (1.0 / d)

    return pl.pallas_call(
        body, grid=(S // spb,),
        in_specs=[pl.BlockSpec((8, D), lambda i: (i, 0)), pl.BlockSpec(memory_space=pl.ANY)],
        out_specs=pl.BlockSpec((8, D), lambda i: (i, 0)), out_shape=jax.ShapeDtypeStruct((T, D), f32),
        scratch_shapes=[pltpu.VMEM((2, ncp, M, hd), f32), pltpu.SemaphoreType.DMA((2, ncp))],
        compiler_params=_params(("arbitrary",), 40), name="mem_attn_sample")(q, cache)


def _cumsum_steps(n):
    k, out = 1, []
    while k < n:
        out.append(k)
        k *= 2
    return out


def _ssd_prompt(xbc, z, small, conv_w, conv_b, dt_bias, a_log, d_skip, ssm_norm, nb):
    T, C = xbc.shape
    d_ssm = z.shape[1]
    H = d_ssm // SSM_HEAD_DIM
    P, N, L = SSM_HEAD_DIM, SSM_STATE, SSD_CHUNK
    hpg = H // SSM_GROUPS
    kw = conv_w.shape[0]
    seq = T // nb
    nc = seq // L
    assert seq % L == 0 and kw - 1 <= 8

    pad = lambda v: jnp.zeros((1, LANES), f32).at[0, :H].set(v)
    col = lambda v: jnp.broadcast_to(v[:, None], (H, LANES))

    rep = (jnp.arange(LANES)[:, None] == jnp.arange(d_ssm)[None, :] // P).astype(bf16)

    def body(xbc_ref, z_ref, sm_ref, cw_ref, cb_ref, dtb_r, alog_r, dskx_ref, dtb_c, alog_c, nrm_ref, rep_ref,
             y_ref, st_ref, cv_ref, xpad, hst):
        c = pl.program_id(1)

        @pl.when(c == 0)
        def _():
            xpad[0:8] = jnp.zeros((8, C), f32)
            hst[...] = jnp.zeros_like(hst)

        xpad[8:8 + L] = xbc_ref[...]
        acc = cb_ref[...] + cw_ref[0:1] * xpad[9 - kw:9 - kw + L]
        for k in range(1, kw):
            acc += cw_ref[k:k + 1] * xpad[9 - kw + k:9 - kw + k + L]
        xc = jax.nn.silu(acc)
        xpad[0:8] = xpad[L:L + 8]

        @pl.when(c == nc - 1)
        def _():
            cv_ref[...] = xpad[9 - kw:8]

        sm = sm_ref[...]
        dt_c = jax.nn.softplus(sm + dtb_r[...])
        acs_c = dt_c * (-jnp.exp(alog_r[...]))
        row = lax.broadcasted_iota(jnp.int32, (L, LANES), 0)
        for k in _cumsum_steps(L):
            acs_c = acs_c + jnp.where(row >= k, pltpu.roll(acs_c, k, 0), 0.0)
        dt_r = jax.nn.softplus(sm.T[0:H] + dtb_c[...])
        acs_r = dt_r * (-jnp.exp(alog_c[...]))
        lane = lax.broadcasted_iota(jnp.int32, (H, L), 1)
        for k in _cumsum_steps(L):
            acs_r = acs_r + jnp.where(lane >= k, pltpu.roll(acs_r, k, 1), 0.0)

        def per_head_lanes(v):
            hi, lo = _split2(v)
            return _nn(hi, rep_ref[...]) + _nn(lo, rep_ref[...])

        alast_c = acs_c[L - 1:L, :]
        xs_all = xc[:, 0:d_ssm]
        xdt_all = xs_all * per_head_lanes(dt_c)
        xd_all = (xdt_all * per_head_lanes(jnp.exp(alast_c - acs_c))).astype(bf16)
        xdt_bf = xdt_all.astype(bf16)
        causal = lax.broadcasted_iota(jnp.int32, (L, L), 0) >= lax.broadcasted_iota(jnp.int32, (L, L), 1)
        ydiags, yoffs = [], []
        for g in range(SSM_GROUPS):
            bg = xc[:, d_ssm + g * N:d_ssm + (g + 1) * N].astype(bf16)
            cg = xc[:, d_ssm + (SSM_GROUPS + g) * N:d_ssm + (SSM_GROUPS + g + 1) * N].astype(bf16)
            cb = _nt(cg, bg)
            hg = hst[g * hpg:(g + 1) * hpg].reshape(hpg * P, N)
            yoffs.append(_nt(cg, hg.astype(bf16)))
            for c4 in range(hpg):
                h = g * hpg + c4
                dec = jnp.exp(jnp.where(causal, acs_c[:, h:h + 1] - acs_r[h:h + 1, :], NEG))
                ydiags.append(_nn((cb * dec).astype(bf16), xdt_bf[:, h * P:(h + 1) * P]))
            dh = _tn(xd_all[:, g * hpg * P:(g + 1) * hpg * P], bg)
            for c4 in range(hpg):
                h = g * hpg + c4
                hst[h] = hst[h] * jnp.exp(alast_c[:, h:h + 1]) + dh[c4 * P:(c4 + 1) * P]

        y = (jnp.concatenate(ydiags, axis=1) + jnp.concatenate(yoffs, axis=1) * per_head_lanes(jnp.exp(acs_c))
             + dskx_ref[...] * xs_all)
        y_ref[...] = _rms(y * jax.nn.silu(z_ref[...]), nrm_ref[...])

        @pl.when(c == nc - 1)
        def _():
            st_ref[...] = hst[...]

    tok = lambda w: pl.BlockSpec((L, w), lambda b, c: (b * nc + c, 0))
    cst = lambda shp: pl.BlockSpec(shp, lambda b, c: tuple(0 for _ in shp))
    return pl.pallas_call(
        body, grid=(nb, nc),
        in_specs=[tok(C), tok(d_ssm), tok(LANES), cst((kw, C)), cst((1, C)), cst((1, LANES)), cst((1, LANES)),
                  cst((1, d_ssm)), cst((H, LANES)), cst((H, LANES)), cst((1, d_ssm)), cst((LANES, d_ssm))],
        out_specs=[tok(d_ssm), pl.BlockSpec((None, H, P, N), lambda b, c: (b, 0, 0, 0)),
                   pl.BlockSpec((None, kw - 1, C), lambda b, c: (b, 0, 0))],
        out_shape=[jax.ShapeDtypeStruct((T, d_ssm), f32), jax.ShapeDtypeStruct((nb, H, P, N), f32),
                   jax.ShapeDtypeStruct((nb, kw - 1, C), f32)],
        scratch_shapes=[pltpu.VMEM((L + 8, C), f32), pltpu.VMEM((H, P, N), f32)],
        compiler_params=_params(("parallel", "arbitrary"), 32), name="ssd_prompt",
    )(xbc, z, small, conv_w, conv_b.reshape(1, C), pad(dt_bias), pad(a_log), jnp.repeat(d_skip, P).reshape(1, d_ssm),
      col(dt_bias), col(a_log), ssm_norm.reshape(1, d_ssm), rep)


def _ssd_sample(xbc, z, small, state, conv_prev, layer, conv_w, conv_b, dt_bias, a_log, d_skip, ssm_norm, dec_seq):
    T, C = xbc.shape
    S = T // dec_seq
    d_ssm = z.shape[1]
    H = d_ssm // SSM_HEAD_DIM
    P, N = SSM_HEAD_DIM, SSM_STATE
    hpg = H // SSM_GROUPS
    kw = conv_w.shape[0]
    Lt = dec_seq
    tmaj = lambda a: a.reshape(S, Lt, a.shape[1]).transpose(1, 0, 2)

    def conv_body(x_ref, prev_ref, sm_ref, cw_ref, cb_ref, xt_ref, smt_ref, cv_ref):
        rows = [prev_ref[k] for k in range(kw - 1)] + [x_ref[t] for t in range(Lt)]
        for t in range(Lt):
            acc = cb_ref[...] + cw_ref[0:1] * rows[t]
            for k in range(1, kw):
                acc += cw_ref[k:k + 1] * rows[t + k]
            xt_ref[t] = jax.nn.silu(acc).T
            smt_ref[t] = sm_ref[t].T
        for k in range(kw - 1):
            cv_ref[k] = rows[Lt + k]

    xt, smt, conv_new = pl.pallas_call(
        conv_body, grid=(1,),
        in_specs=[pl.BlockSpec((Lt, S, C), lambda i: (0, 0, 0)), pl.BlockSpec((None, kw - 1, S, C), lambda i: (layer, 0, 0, 0)),
                  pl.BlockSpec((Lt, S, LANES), lambda i: (0, 0, 0)), pl.BlockSpec((kw, C), lambda i: (0, 0)),
                  pl.BlockSpec((1, C), lambda i: (0, 0))],
        out_specs=[pl.BlockSpec((Lt, C, S), lambda i: (0, 0, 0)), pl.BlockSpec((Lt, LANES, S), lambda i: (0, 0, 0)),
                   pl.BlockSpec((kw - 1, S, C), lambda i: (0, 0, 0))],
        out_shape=[jax.ShapeDtypeStruct((Lt, C, S), f32), jax.ShapeDtypeStruct((Lt, LANES, S), f32),
                   jax.ShapeDtypeStruct((kw - 1, S, C), f32)],
        compiler_params=_params(("arbitrary",), 32), name="ssd_sample_conv",
    )(tmaj(xbc), conv_prev, tmaj(small), conv_w, conv_b.reshape(1, C))

    def state_body(st_ref, xt_ref, smt_ref, dtb_ref, alog_ref, dsk_ref, nst_ref, y_ref, xdt_s):
        h = pl.program_id(0)
        g = h // hpg
        xoff = pl.multiple_of(h * P, P)
        boff = pl.multiple_of(d_ssm + g * N, N)
        coff = pl.multiple_of(d_ssm + (SSM_GROUPS + g) * N, N)
        a = -jnp.exp(alog_ref[pl.ds(h, 1), :])
        dsk = dsk_ref[pl.ds(h, 1), :]
        decs, bs, cs = [], [], []
        for t in range(Lt):
            dt = jax.nn.softplus(smt_ref[t, pl.ds(h, 1), :] + dtb_ref[pl.ds(h, 1), :])
            x = xt_ref[t, pl.ds(xoff, P), :]
            decs.append(jnp.exp(dt * a))
            xdt_s[t] = x * dt
            bs.append(xt_ref[t, pl.ds(boff, N), :])
            cs.append(xt_ref[t, pl.ds(coff, N), :])
            y_ref[t] = dsk * x

        def step(p, carry):
            r0 = pl.multiple_of(p * N, N)
            hp = st_ref[pl.ds(r0, N), :]
            for t in range(Lt):
                hp = hp * decs[t] + xdt_s[t, pl.ds(p, 1), :] * bs[t]
                y_ref[t, pl.ds(p, 1), :] += jnp.sum(cs[t] * hp, axis=0, keepdims=True)
            nst_ref[pl.ds(r0, N), :] = hp
            return carry

        lax.fori_loop(0, P, step, 0)

    colS = lambda v: jnp.broadcast_to(v[:, None], (H, S))
    new_state, yt = pl.pallas_call(
        state_body, grid=(H,),
        in_specs=[pl.BlockSpec((None, None, P * N, S), lambda h: (layer, h, 0, 0)),
                  pl.BlockSpec((Lt, C, S), lambda h: (0, 0, 0)), pl.BlockSpec((Lt, LANES, S), lambda h: (0, 0, 0)),
                  pl.BlockSpec((H, S), lambda h: (0, 0)), pl.BlockSpec((H, S), lambda h: (0, 0)),
                  pl.BlockSpec((H, S), lambda h: (0, 0))],
        out_specs=[pl.BlockSpec((None, P * N, S), lambda h: (h, 0, 0)), pl.BlockSpec((Lt, P, S), lambda h: (0, h, 0))],
        out_shape=[jax.ShapeDtypeStruct((H, P * N, S), f32), jax.ShapeDtypeStruct((Lt, d_ssm, S), f32)],
        scratch_shapes=[pltpu.VMEM((Lt, P, S), f32)],
        compiler_params=_params(("parallel",), 32), name="ssd_sample_state",
    )(state, xt, smt, colS(dt_bias), colS(a_log), colS(d_skip))

    def out_body(yt_ref, z_ref, nrm_ref, o_ref):
        for t in range(Lt):
            y = yt_ref[t].T * jax.nn.silu(z_ref[t])
            o_ref[t] = _rms(y, nrm_ref[...])

    y = pl.pallas_call(
        out_body, grid=(1,),
        in_specs=[pl.BlockSpec((Lt, d_ssm, S), lambda i: (0, 0, 0)), pl.BlockSpec((Lt, S, d_ssm), lambda i: (0, 0, 0)),
                  pl.BlockSpec((1, d_ssm), lambda i: (0, 0))],
        out_specs=pl.BlockSpec((Lt, S, d_ssm), lambda i: (0, 0, 0)),
        out_shape=jax.ShapeDtypeStruct((Lt, S, d_ssm), f32),
        compiler_params=_params(("arbitrary",), 32), name="ssd_sample_out",
    )(yt, tmaj(z), ssm_norm.reshape(1, d_ssm))
    return y.transpose(1, 0, 2).reshape(T, d_ssm), new_state, conv_new


def _slot_block(slot):
    return jnp.where(slot < LANES // 2, 2 * slot, 2 * (slot - LANES // 2) + 1)


def _pool_consts(cmp_wk, cmp_wv, rows):
    r = jnp.arange(rows)
    tile = lambda w: jnp.tile(w.T[:, r % CMP_BLOCK], (N_KV, 1))
    wt = jnp.concatenate([tile(cmp_wk), tile(cmp_wv)], axis=0)
    blk = r // CMP_BLOCK
    slot = jnp.where(blk % 2 == 0, blk // 2, LANES // 2 + blk // 2)
    seg = (slot[:, None] == jnp.arange(LANES)[None, :]).astype(bf16)
    return wt, seg


def _expand_const(n_keys):
    blk = jnp.arange(n_keys) // SEL_BLOCK
    return (jnp.arange(LANES)[:, None] == blk[None, :]).astype(bf16)


def _cmp_pool_prompt(kvt, wt, seg):
    nb, _, L = kvt.shape
    F = 2 * N_KV * HEAD_DIM

    def body(kv_ref, wt_ref, seg_ref, segt_ref, pool_ref, poolt_ref):
        hi, lo = _split2(kv_ref[...] * wt_ref[...])
        poolt_ref[...] = _nn(hi, seg_ref[...]) + _nn(lo, seg_ref[...])
        pool_ref[...] = _nt(segt_ref[...], hi) + _nt(segt_ref[...], lo)

    return pl.pallas_call(
        body, grid=(nb,),
        in_specs=[pl.BlockSpec((None, F, L), lambda b: (b, 0, 0)), pl.BlockSpec((F, L), lambda b: (0, 0)),
                  pl.BlockSpec((L, LANES), lambda b: (0, 0)), pl.BlockSpec((LANES, L), lambda b: (0, 0))],
        out_specs=[pl.BlockSpec((None, LANES, F), lambda b: (b, 0, 0)), pl.BlockSpec((None, F, LANES), lambda b: (b, 0, 0))],
        out_shape=[jax.ShapeDtypeStruct((nb, LANES, F), f32), jax.ShapeDtypeStruct((nb, F, LANES), f32)],
        compiler_params=_params(("parallel",), 32), name="cmp_pool_prompt")(kvt, wt, seg, seg.T)


def _select_blocks(score, valid, blk, n_blocks, axis):
    cnt = jnp.zeros(score.shape, f32)
    for i in range(n_blocks):
        si = score[i:i + 1, :] if axis == 0 else score[:, i:i + 1]
        beat = (si > score) | ((si == score) & (blk > i))
        cnt += jnp.where(beat, 1.0, 0.0)
    return jnp.where((cnt < TOP_N) & valid, 1.0, 0.0)


def _nsa_prompt(q, small, kvt, wint, pool, poolt, expand, gate_b, goff, tq=128):
    T, dq = q.shape
    nb, _, L = kvt.shape
    hd = HEAD_DIM
    H = dq // hd
    hpg = H // N_KV
    F = 2 * N_KV * hd
    nq = L // tq
    scale = hd ** -0.5
    n_sel = L // SEL_BLOCK
    span = tq + WINDOW
    assert L % SEL_TK == 0 and L % tq == 0 and tq % SEL_BLOCK == 0 and n_sel <= LANES // 2 and WINDOW % LANES == 0

    def body(q_ref, sm_ref, kv_ref, win_ref, pool_ref, poolt_ref, e_ref, gb_ref, o_ref, kbf, wbf, bias_ref):
        qi = pl.program_id(1)

        @pl.when(qi == 0)
        def _():
            kbf[...] = kv_ref[...].astype(bf16)
            wbf[:, 0:WINDOW] = jnp.zeros((F, WINDOW), bf16)
            wbf[:, WINDOW:] = win_ref[...].astype(bf16)

        q0 = pl.multiple_of(qi * tq, tq)
        qs = q_ref[...] * scale
        gates = jax.nn.sigmoid(sm_ref[...] + gb_ref[...])
        qpos_c = q0 + lax.broadcasted_iota(jnp.int32, (tq, 1), 0)
        qpos_r = q0 + lax.broadcasted_iota(jnp.int32, (1, tq), 1)
        qpos_c4 = jnp.concatenate([qpos_c] * hpg, axis=0)
        qpos_r4 = jnp.concatenate([qpos_r] * hpg, axis=1)
        slot_r = _slot_block(lax.broadcasted_iota(jnp.int32, (1, LANES), 1))
        slot_c = _slot_block(lax.broadcasted_iota(jnp.int32, (LANES, 1), 0))
        outs = []
        for g in range(N_KV):
            qg = jnp.concatenate([qs[:, (g * hpg + c) * hd:(g * hpg + c + 1) * hd] for c in range(hpg)], axis=0).astype(bf16)
            kct = poolt_ref[g * hd:(g + 1) * hd, :].astype(bf16)
            kc = pool_ref[:, g * hd:(g + 1) * hd].astype(bf16)
            vc = pool_ref[:, (N_KV + g) * hd:(N_KV + g + 1) * hd].astype(bf16)
            e_cmp, r_cmp = _masked_exp(_nn(qg, kct), (slot_r * CMP_BLOCK + CMP_BLOCK - 1) <= qpos_c4)
            o_cmp = _nn(e_cmp.astype(bf16), vc)
            pt = _masked_softmax(_nt(kc, qg), (slot_c * CMP_BLOCK + CMP_BLOCK - 1) <= qpos_r4, axis=0)
            impt = pt[:, 0:tq]
            for c in range(1, hpg):
                impt = impt + pt[:, c * tq:(c + 1) * tq]
            imp = impt[0:n_sel] + impt[LANES // 2:LANES // 2 + n_sel]
            blk = lax.broadcasted_iota(jnp.int32, (n_sel, 1), 0)
            cur = qpos_r // SEL_BLOCK
            valid = blk <= cur
            forced = (blk == 0) | (blk == cur) | (blk == cur - 1)
            score = jnp.where(valid, jnp.where(forced, FORCE_SCORE, imp), NEG)
            selt = _select_blocks(score, valid, blk, n_sel, axis=0)
            sel = jnp.concatenate([selt, jnp.zeros((LANES - n_sel, tq), f32)], axis=0).T
            expd = _nn(sel.astype(bf16), e_ref[...])
            kpos = lax.broadcasted_iota(jnp.int32, (1, L), 1)
            bias_ref[...] = jnp.where((expd > 0.5) & (kpos <= qpos_c), 0.0, NEG)

            def sel_step(j, carry):
                m, l, acc = carry
                k0 = pl.multiple_of(j * SEL_TK, SEL_TK)
                kt = kbf[g * hd:(g + 1) * hd, pl.ds(k0, SEL_TK)]
                vt = kbf[(N_KV + g) * hd:(N_KV + g + 1) * hd, pl.ds(k0, SEL_TK)]
                s = _nn(qg, kt).reshape(hpg, tq, SEL_TK) + bias_ref[:, pl.ds(k0, SEL_TK)][None]
                s = s.reshape(hpg * tq, SEL_TK)
                m_new = jnp.maximum(m, jnp.max(s, axis=-1, keepdims=True))
                alpha = jnp.exp(m - m_new)
                pj = jnp.exp(s - m_new)
                l = alpha * l + jnp.sum(pj, axis=-1, keepdims=True)
                acc = alpha * acc + _nt(pj.astype(bf16), vt)
                return m_new, l, acc

            init = (jnp.full((hpg * tq, 1), NEG, f32), jnp.zeros((hpg * tq, 1), f32), jnp.zeros((hpg * tq, hd), f32))
            _, l_sel, o_sel = lax.fori_loop(0, (q0 + tq + SEL_TK - 1) // SEL_TK, sel_step, init)
            r_sel = 1.0 / l_sel
            kw_ = wbf[g * hd:(g + 1) * hd, pl.ds(q0, span)]
            vw_ = wbf[(N_KV + g) * hd:(N_KV + g + 1) * hd, pl.ds(q0, span)]
            col = lax.broadcasted_iota(jnp.int32, (1, span), 1)
            dist = (lax.broadcasted_iota(jnp.int32, (tq, 1), 0) + WINDOW) - col
            wmask = (dist >= 0) & (dist < WINDOW) & (col + q0 >= WINDOW)
            wmask4 = jnp.concatenate([wmask] * hpg, axis=0)
            e_win, r_win = _masked_exp(_nn(qg, kw_), wmask4)
            o_win = _nt(e_win.astype(bf16), vw_)
            for c in range(hpg):
                h = g * hpg + c
                rows = slice(c * tq, (c + 1) * tq)
                gc, gs, gw = (gates[:, goff + k * H + h:goff + k * H + h + 1] for k in range(3))
                outs.append((gc * r_cmp[rows]) * o_cmp[rows] + (gs * r_sel[rows]) * o_sel[rows] + (gw * r_win[rows]) * o_win[rows])
        o_ref[...] = jnp.concatenate(outs, axis=1)

    tok = lambda w: pl.BlockSpec((tq, w), lambda b, i: (b * nq + i, 0))
    return pl.pallas_call(
        body, grid=(nb, nq),
        in_specs=[tok(dq), tok(LANES), pl.BlockSpec((None, F, L), lambda b, i: (b, 1, 0)),
                  pl.BlockSpec((None, F, L), lambda b, i: (b, 0, 0)),
                  pl.BlockSpec((None, LANES, F), lambda b, i: (b, 0, 0)), pl.BlockSpec((None, F, LANES), lambda b, i: (b, 0, 0)),
                  pl.BlockSpec((LANES, L), lambda b, i: (0, 0)), pl.BlockSpec((1, LANES), lambda b, i: (0, 0))],
        out_specs=tok(dq), out_shape=jax.ShapeDtypeStruct((T, dq), f32),
        scratch_shapes=[pltpu.VMEM((F, L), bf16), pltpu.VMEM((F, L + WINDOW), bf16), pltpu.VMEM((tq, L), f32)],
        compiler_params=_params(("parallel", "arbitrary"), 48), name="nsa_prompt",
    )(q, small, kvt, wint, pool, poolt, expand, gate_b)


def _nsa_sample(q, kv_new, win_new, small, cache_t, win_t, page_table, layer, wt_page, seg, expand, gate_b, goff, dec_seq):
    T, dq = q.shape
    S, n_pages = page_table.shape
    page = cache_t.shape[-1]
    keep = win_t.shape[-1]
    past = n_pages * page
    hd = HEAD_DIM
    H = dq // hd
    hpg = H // N_KV
    F = 2 * N_KV * hd
    Lt = dec_seq
    spb = 8 // Lt
    nr = hpg * Lt
    scale = hd ** -0.5
    n_sel = past // SEL_BLOCK + 1
    new_blk = past // SEL_BLOCK
    k_start = past - keep
    assert 8 % Lt == 0 and S % spb == 0 and past % SEL_BLOCK == 0 and Lt <= SEL_BLOCK and n_sel <= LANES // 2
    assert page == LANES and keep % LANES == 0 and keep <= WINDOW and Lt <= LANES

    def body(pt_ref, q_ref, kvn_ref, wn_ref, sm_ref, *refs):
        win_ref, wt_ref, seg_ref, e_ref, gb_ref, shift_ref, o_ref, wout_ref = refs[spb * n_pages:]
        gates_all = jax.nn.sigmoid(sm_ref[...] + gb_ref[...])
        for sq in range(spb):
            one_sequence(sq, refs[sq * n_pages:(sq + 1) * n_pages], q_ref, kvn_ref, wn_ref, gates_all, win_ref, wt_ref, seg_ref,
                         e_ref, shift_ref, o_ref, wout_ref)

    def one_sequence(sq, pages, q_ref, kvn_ref, wn_ref, gates_all, win_ref, wt_ref, seg_ref, e_ref, shift_ref, o_ref, wout_ref):
        srows = slice(sq * Lt, (sq + 1) * Lt)
        qs = q_ref[srows, :] * scale
        kvn = kvn_ref[srows, :]
        wn = wn_ref[srows, :]
        gates = gates_all[srows]

        hi, lo = _split2(jnp.concatenate([pages[p][0:F, :] * wt_ref[...] for p in range(n_pages)], axis=1))
        poolt = _nn(hi, seg_ref[...]) + _nn(lo, seg_ref[...])

        tpos = lax.broadcasted_iota(jnp.int32, (Lt, 1), 0)
        qpos = past + tpos
        qpos_n = jnp.concatenate([qpos] * hpg, axis=0)
        tpos_n = jnp.concatenate([tpos] * hpg, axis=0)
        lane = lax.broadcasted_iota(jnp.int32, (1, LANES), 1)
        slot_r = _slot_block(lane)
        outs = []
        for g in range(N_KV):
            qg32 = jnp.concatenate([qs[:, (g * hpg + c) * hd:(g * hpg + c + 1) * hd] for c in range(hpg)], axis=0)
            qg = qg32.astype(bf16)
            p = _masked_softmax(_nn(qg, poolt[g * hd:(g + 1) * hd].astype(bf16)), (slot_r * CMP_BLOCK + CMP_BLOCK - 1) <= qpos_n)
            o_cmp = _nt(p.astype(bf16), poolt[(N_KV + g) * hd:(N_KV + g + 1) * hd].astype(bf16))
            imp = p[0:Lt]
            for c in range(1, hpg):
                imp = imp + p[c * Lt:(c + 1) * Lt]
            imp = imp + pltpu.roll(imp, LANES // 2, 1)
            cur = qpos // SEL_BLOCK
            valid = (lane <= cur) & (lane < n_sel)
            forced = (lane == 0) | (lane == cur) | (lane == cur - 1)
            score = jnp.where(valid, jnp.where(forced, FORCE_SCORE, imp), NEG)
            sel = _select_blocks(score, valid, lane, n_sel, axis=1)
            expd = _nn(sel.astype(bf16), e_ref[...])
            kpos = lax.broadcasted_iota(jnp.int32, (1, past), 1)
            bias = jnp.where((expd > 0.5) & (kpos <= qpos), 0.0, NEG)
            bias_n = jnp.concatenate([bias] * hpg, axis=0)
            sel_new = jnp.concatenate([sel[:, new_blk:new_blk + 1]] * hpg, axis=0) > 0.5

            def new_scores(k_rows, ok):
                cols = []
                for t in range(Lt):
                    s_t = jnp.sum(qg32 * k_rows[t:t + 1, :], axis=-1, keepdims=True)
                    cols.append(jnp.where(ok & (tpos_n >= t), s_t, NEG))
                return cols

            def attend(s_past, s_new, v_past_fn, v_new):
                m = jnp.max(s_past, axis=-1, keepdims=True)
                for s_t in s_new:
                    m = jnp.maximum(m, s_t)
                e_past = jnp.exp(s_past - m)
                den = jnp.sum(e_past, axis=-1, keepdims=True)
                acc = v_past_fn(e_past.astype(bf16))
                for t, s_t in enumerate(s_new):
                    e_t = jnp.exp(s_t - m)
                    den = den + e_t
                    acc = acc + e_t * v_new[t:t + 1, :]
                return acc, 1.0 / den

            s_past = jnp.concatenate([_nn(qg, pages[p][(2 * N_KV + g) * hd:(2 * N_KV + g + 1) * hd, :].astype(bf16))
                                      for p in range(n_pages)], axis=1) + bias_n

            def v_sel(e):
                acc = _nt(e[:, 0:page], pages[0][(3 * N_KV + g) * hd:(3 * N_KV + g + 1) * hd, :].astype(bf16))
                for p in range(1, n_pages):
                    acc += _nt(e[:, p * page:(p + 1) * page], pages[p][(3 * N_KV + g) * hd:(3 * N_KV + g + 1) * hd, :].astype(bf16))
                return acc

            o_sel, r_sel = attend(s_past, new_scores(kvn[:, (2 * N_KV + g) * hd:(2 * N_KV + g + 1) * hd], sel_new),
                                  v_sel, kvn[:, (3 * N_KV + g) * hd:(3 * N_KV + g + 1) * hd])
            kpw = k_start + lax.broadcasted_iota(jnp.int32, (1, keep), 1)
            dist = qpos_n - kpw
            s_w = jnp.where((dist >= 0) & (dist < WINDOW), _nn(qg, win_ref[sq, g * hd:(g + 1) * hd, :].astype(bf16)), NEG)
            always = jnp.full((nr, 1), True)
            o_win, r_win = attend(s_w, new_scores(wn[:, g * hd:(g + 1) * hd], always),
                                  lambda e: _nt(e, win_ref[sq, (N_KV + g) * hd:(N_KV + g + 1) * hd, :].astype(bf16)),
                                  wn[:, (N_KV + g) * hd:(N_KV + g + 1) * hd])
            for c in range(hpg):
                h = g * hpg + c
                rows = slice(c * Lt, (c + 1) * Lt)
                gc, gs, gw = (gates[:, goff + k * H + h:goff + k * H + h + 1] for k in range(3))
                outs.append(gc * o_cmp[rows] + (gs * r_sel[rows]) * o_sel[rows] + (gw * r_win[rows]) * o_win[rows])
        o_ref[srows, :] = jnp.concatenate(outs, axis=1)

        rolled = pltpu.roll(win_ref[sq], keep - Lt, 1)
        hi, mid, lo = _split3(wn)
        new_t = _tn(hi, shift_ref[...]) + _tn(mid, shift_ref[...]) + _tn(lo, shift_ref[...])
        last = jnp.where(lane >= LANES - Lt, new_t, rolled[:, keep - LANES:keep])
        if keep > LANES:
            wout_ref[sq, :, 0:keep - LANES] = rolled[:, 0:keep - LANES]
        wout_ref[sq, :, keep - LANES:keep] = last

    shift = (jnp.arange(Lt)[:, None] + (LANES - Lt) == jnp.arange(LANES)[None, :]).astype(bf16)
    tok = lambda w: pl.BlockSpec((8, w), lambda i, pt: (i, 0))
    cst = lambda shp: pl.BlockSpec(shp, lambda i, pt: tuple(0 for _ in shp))
    page_spec = lambda sq, p: pl.BlockSpec((None, None, 2 * F, page), lambda i, pt: (layer, pt[i * spb + sq, p], 0, 0))
    wd = win_t.shape[2]
    grid_spec = pltpu.PrefetchScalarGridSpec(
        num_scalar_prefetch=1, grid=(S // spb,),
        in_specs=[tok(dq), tok(2 * F), tok(wd), tok(LANES)] + [page_spec(sq, p) for sq in range(spb) for p in range(n_pages)] + [
            pl.BlockSpec((None, spb, wd, keep), lambda i, pt: (layer, i, 0, 0)),
            cst((F, page)), cst((past, LANES)), cst((LANES, past)), cst((1, LANES)), cst((Lt, LANES))],
        out_specs=[tok(dq), pl.BlockSpec((spb, wd, keep), lambda i, pt: (i, 0, 0))])
    return pl.pallas_call(
        body, grid_spec=grid_spec,
        out_shape=[jax.ShapeDtypeStruct((T, dq), f32), jax.ShapeDtypeStruct((S, wd, keep), f32)],
        compiler_params=_params(("parallel",), 48), name="nsa_sample",
    )(page_table, q, kv_new, win_new, small, *([cache_t] * (spb * n_pages)), win_t, wt_page, seg, expand, gate_b, shift)


def kernel(x_prompt, x_sample, mem_prompt, cache_nsa_kv, cache_mem_kv, state_nsa_win, state_ssm, state_conv, page_table, norm_mix_pre, w_in, conv_w, conv_b, dt_bias, a_log, d_skip, ssm_norm, cmp_wk, cmp_wv, gate_b, w_out, norm_mix_post, norm_mem_src, w_mem_q, w_mem_k, w_mem_v, w_mem_o, norm_mem_pre, norm_mem_post, norm_ffn_pre, w_up, w_down, norm_ffn_post):
    B, L, D = x_prompt.shape
    S, Lt, _ = x_sample.shape
    depth = w_in.shape[0]
    M = mem_prompt.shape[1]
    d_ssm = ssm_norm.shape[1]
    C = conv_w.shape[2]
    H = dt_bias.shape[1]
    P, N = state_ssm.shape[3], state_ssm.shape[4]
    d_att = w_out.shape[1] - d_ssm
    Ha = d_att // HEAD_DIM
    n_pool, page = cache_nsa_kv.shape[1], cache_nsa_kv.shape[2]
    keep = state_nsa_win.shape[2]
    kvw = 4 * N_KV * HEAD_DIM
    winw = 2 * N_KV * HEAD_DIM
    past = page_table.shape[1] * page
    assert H + 3 * Ha <= LANES

    cache_t = cache_nsa_kv.transpose(0, 1, 3, 4, 5, 2).reshape(depth, n_pool, kvw, page)
    win_t = state_nsa_win.transpose(0, 1, 3, 4, 5, 2).reshape(depth, S, winw, keep)
    ssm_t = state_ssm.transpose(0, 2, 3, 4, 1).reshape(depth, H, P * N, S)
    conv_t = state_conv.transpose(0, 2, 1, 3)

    o1 = d_ssm
    o2 = o1 + C
    o3 = o2 + H
    o4 = o3 + d_att
    o5 = o4 + kvw
    o6 = o5 + winw
    n_gate = 3 * Ha

    expand_p = _expand_const(L)
    expand_s = _expand_const(past)
    xp = x_prompt.reshape(B * L, D)
    xs = x_sample.reshape(S * Lt, D)
    mem = mem_prompt.reshape(B * M, D)
    kv_p, kv_s, win_p, win_s, ssm_p, ssm_s, conv_p, conv_s, mkv_out = [], [], [], [], [], [], [], [], []
    for l in range(depth):
        wl = w_in[l].astype(bf16)
        w_z, w_xbc, w_q = wl[:, :o1], wl[:, o1:o2], wl[:, o3:o4]
        w_kv, w_win = wl[:, o4:o5], wl[:, o5:o6]
        w_small = jnp.concatenate([wl[:, o2:o3], wl[:, o6:o6 + n_gate], jnp.zeros((D, LANES - H - n_gate), bf16)], axis=1)
        gb = jnp.zeros((1, LANES), f32).at[0, H:H + n_gate].set(gate_b[l])
        wo = w_out[l].astype(bf16)
        wt_full, seg_full = _pool_consts(cmp_wk[l], cmp_wv[l], L)
        wt_page, seg_page = _pool_consts(cmp_wk[l], cmp_wv[l], past)

        mkv, = _norm_proj(mem, norm_mem_src[l], [jnp.concatenate([w_mem_k[l], w_mem_v[l]], axis=1).astype(bf16)])
        mkv_out.append(mkv.reshape(B, M, 2, MEM_HEADS, D // MEM_HEADS))

        z, xbc, q, small, kvt, wint = _norm_proj(xp, norm_mix_pre[l], [w_z, w_xbc, w_q, w_small], [w_kv.T, w_win.T], seq=L)
        y_ssd, st_new, cv_new = _ssd_prompt(xbc, z, small, conv_w[l], conv_b[l], dt_bias[l], a_log[l], d_skip[l], ssm_norm[l], B)
        pool, poolt = _cmp_pool_prompt(kvt, wt_full, seg_full)
        o_att = _nsa_prompt(q, small, kvt, wint, pool, poolt, expand_p, gb, H)
        xp = _linear_post(xp, [y_ssd, o_att], [wo[:d_ssm], wo[d_ssm:]], norm_mix_post[l])
        kv_p.append(kvt)
        win_p.append(wint[:, :, L - min(WINDOW, L):])
        ssm_p.append(st_new)
        conv_p.append(cv_new)

        z, xbc, q, small, kvr, winr = _norm_proj(xs, norm_mix_pre[l], [w_z, w_xbc, w_q, w_small, w_kv, w_win])
        y_ssd, st_new, cv_new = _ssd_sample(xbc, z, small, ssm_t, conv_t, l, conv_w[l], conv_b[l], dt_bias[l], a_log[l],
                                            d_skip[l], ssm_norm[l], Lt)
        o_att, wnew = _nsa_sample(q, kvr, winr, small, cache_t, win_t, page_table, l, wt_page[:, :page], seg_page,
                                  expand_s, gb, H, Lt)
        xs = _linear_post(xs, [y_ssd, o_att], [wo[:d_ssm], wo[d_ssm:]], norm_mix_post[l])
        kv_s.append(kvr)
        win_s.append(wnew)
        ssm_s.append(st_new)
        conv_s.append(cv_new)

        wq, wmo = w_mem_q[l].astype(bf16), w_mem_o[l].astype(bf16)
        qm, = _norm_proj(xp, norm_mem_pre[l], [wq])
        xp = _linear_post(xp, [_mem_attn_prompt(qm, mkv, L)], [wmo], norm_mem_post[l])
        qm, = _norm_proj(xs, norm_mem_pre[l], [wq])
        xs = _linear_post(xs, [_mem_attn_sample(qm, cache_mem_kv, l, Lt)], [wmo], norm_mem_post[l])

        wu, wd = w_up[l].astype(bf16), w_down[l].astype(bf16)
        xp = _ffn(xp, norm_ffn_pre[l], wu, wd, norm_ffn_post[l], tm=1024)
        xs = _ffn(xs, norm_ffn_pre[l], wu, wd, norm_ffn_post[l], tm=512)

    kv5 = (4, N_KV, HEAD_DIM)
    w5 = (2, N_KV, HEAD_DIM)
    return (xp.reshape(B, L, D), xs.reshape(S, Lt, D),
            jnp.stack(kv_p).reshape(depth, B, *kv5, L).transpose(0, 1, 5, 2, 3, 4),
            jnp.stack(kv_s).reshape(depth, S, Lt, *kv5),
            jnp.stack(win_p).reshape(depth, B, *w5, min(WINDOW, L)).transpose(0, 1, 5, 2, 3, 4),
            jnp.stack(win_s).reshape(depth, S, *w5, keep).transpose(0, 1, 5, 2, 3, 4),
            jnp.stack(ssm_p), jnp.stack(ssm_s).reshape(depth, H, P, N, S).transpose(0, 4, 1, 2, 3),
            jnp.stack(conv_p), jnp.stack(conv_s).transpose(0, 2, 1, 3),
            jnp.stack(mkv_out))
```

```python
import functools

import jax
import jax.numpy as jnp
from jax import lax
from jax.experimental import pallas as pl
from jax.experimental.pallas import tpu as pltpu

f32 = jnp.float32
bf16 = jnp.bfloat16
NEG = float("-inf")

SSM_HEAD_DIM = 64
SSM_GROUPS = 2
SSM_STATE = 64
SSD_CHUNK = 128
HEAD_DIM = 64
N_KV = 2
CMP_BLOCK = 32
SEL_BLOCK = 64
TOP_N = 16
WINDOW = 512
FORCE_SCORE = 1.0e4
MEM_HEADS = 4
EPS = 1e-6

LANES = 128
V7X_VMEM_BYTES = 64 * 1024 * 1024
SEL_TK = 512
SAMPLE_SEQS = 4


def _params(sem, vmem_mb):
    assert vmem_mb * 1024 * 1024 < V7X_VMEM_BYTES
    return pltpu.CompilerParams(dimension_semantics=sem, vmem_limit_bytes=vmem_mb * 1024 * 1024)


def _nn(a, b):
    return jnp.dot(a, b, preferred_element_type=f32)


def _nt(a, b):
    return lax.dot_general(a, b, (((1,), (1,)), ((), ())), preferred_element_type=f32)


def _tn(a, b):
    return lax.dot_general(a, b, (((0,), (0,)), ((), ())), preferred_element_type=f32)


def _rms(x, g):
    return x * lax.rsqrt(jnp.mean(x * x, axis=-1, keepdims=True) + EPS) * g


def _masked_exp(s, mask, axis=-1):
    s = jnp.where(mask, s, NEG)
    m = jnp.max(s, axis=axis, keepdims=True)
    m = jnp.where(jnp.isfinite(m), m, 0.0)
    e = jnp.exp(s - m)
    d = jnp.sum(e, axis=axis, keepdims=True)
    return e, 1.0 / jnp.where(d > 0, d, 1.0)


def _masked_softmax(s, mask, axis=-1):
    e, r = _masked_exp(s, mask, axis)
    return e * r


def _split2(x):
    hi = x.astype(bf16)
    lo = (x - hi.astype(f32)).astype(bf16)
    return hi, lo


def _split3(x):
    hi = x.astype(bf16)
    r = x - hi.astype(f32)
    mid = r.astype(bf16)
    lo = (r - mid.astype(f32)).astype(bf16)
    return hi, mid, lo


def _norm_proj(x, g, ws, wts=(), seq=None, tm=512):
    T, D = x.shape
    tm = min(tm, T)
    assert T % tm == 0
    ws, wts = list(ws), list(wts)
    n_row = len(ws)
    tps = (seq // tm) if wts else 1

    def body(x_ref, g_ref, *refs):
        w_refs, o_refs = refs[:len(ws) + len(wts)], refs[len(ws) + len(wts):]
        xn = _rms(x_ref[...], g_ref[...]).astype(bf16)
        for k in range(n_row):
            o_refs[k][...] = _nn(xn, w_refs[k][...])
        for k in range(n_row, len(w_refs)):
            o_refs[k][...] = _nt(w_refs[k][...], xn)

    in_specs = [pl.BlockSpec((tm, D), lambda i: (i, 0)), pl.BlockSpec((1, D), lambda i: (0, 0))]
    in_specs += [pl.BlockSpec(w.shape, lambda i: (0, 0)) for w in ws + wts]
    out_specs = [pl.BlockSpec((tm, w.shape[1]), lambda i: (i, 0)) for w in ws]
    out_specs += [pl.BlockSpec((None, w.shape[0], tm), lambda i: (i // tps, 0, i % tps)) for w in wts]
    out_shape = [jax.ShapeDtypeStruct((T, w.shape[1]), f32) for w in ws]
    out_shape += [jax.ShapeDtypeStruct((T // seq, w.shape[0], seq), f32) for w in wts]
    return pl.pallas_call(body, grid=(T // tm,), in_specs=in_specs, out_specs=out_specs, out_shape=out_shape,
                          compiler_params=_params(("parallel",), 48), name="norm_proj")(x, g.reshape(1, D), *ws, *wts)


def _linear_post(res, acts, ws, g, tm=512):
    T, D = res.shape
    tm = min(tm, T)
    n = len(acts)

    def body(*refs):
        a_refs, w_refs = refs[:n], refs[n:2 * n]
        g_ref, res_ref, o_ref = refs[2 * n:]
        acc = _nn(a_refs[0][...].astype(bf16), w_refs[0][...])
        for k in range(1, n):
            acc += _nn(a_refs[k][...].astype(bf16), w_refs[k][...])
        o_ref[...] = res_ref[...] + _rms(acc, g_ref[...])

    in_specs = [pl.BlockSpec((tm, a.shape[1]), lambda i: (i, 0)) for a in acts]
    in_specs += [pl.BlockSpec(w.shape, lambda i: (0, 0)) for w in ws]
    in_specs += [pl.BlockSpec((1, D), lambda i: (0, 0)), pl.BlockSpec((tm, D), lambda i: (i, 0))]
    return pl.pallas_call(body, grid=(T // tm,), in_specs=in_specs, out_specs=pl.BlockSpec((tm, D), lambda i: (i, 0)),
                          out_shape=jax.ShapeDtypeStruct((T, D), f32),
                          compiler_params=_params(("parallel",), 40), name="linear_post")(*acts, *ws, g.reshape(1, D), res)


def _ffn(x, g1, wu, wd, g2, tm, tc=512):
    T, D = x.shape
    F = wu.shape[1]
    tm = min(tm, T)
    nj = F // tc

    def body(x_ref, g1_ref, wu_ref, wd_ref, g2_ref, o_ref, xn_s, acc_s):
        j = pl.program_id(1)

        @pl.when(j == 0)
        def _():
            xn_s[...] = _rms(x_ref[...], g1_ref[...]).astype(bf16)
            acc_s[...] = jnp.zeros_like(acc_s)

        h = _nn(xn_s[...], wu_ref[...])
        h = jnp.square(jnp.maximum(h, 0.0))
        acc_s[...] += _nn(h.astype(bf16), wd_ref[...])

        @pl.when(j == nj - 1)
        def _():
            o_ref[...] = x_ref[...] + _rms(acc_s[...], g2_ref[...])

    return pl.pallas_call(
        body, grid=(T // tm, nj),
        in_specs=[pl.BlockSpec((tm, D), lambda i, j: (i, 0)), pl.BlockSpec((1, D), lambda i, j: (0, 0)),
                  pl.BlockSpec((D, tc), lambda i, j: (0, j)), pl.BlockSpec((tc, D), lambda i, j: (j, 0)),
                  pl.BlockSpec((1, D), lambda i, j: (0, 0))],
        out_specs=pl.BlockSpec((tm, D), lambda i, j: (i, 0)),
        out_shape=jax.ShapeDtypeStruct((T, D), f32),
        scratch_shapes=[pltpu.VMEM((tm, D), bf16), pltpu.VMEM((tm, D), f32)],
        compiler_params=_params(("parallel", "arbitrary"), 48), name="ffn")(x, g1.reshape(1, D), wu, wd, g2.reshape(1, D))


def _mem_attn_prompt(q, mkv, seq, tq=512):
    T, D = q.shape
    M = mkv.shape[0] // (T // seq)
    hd = D // MEM_HEADS
    scale = hd ** -0.5
    tps = seq // tq

    def body(q_ref, kv_ref, o_ref):
        for h in range(MEM_HEADS):
            qh = (q_ref[:, h * hd:(h + 1) * hd] * scale).astype(bf16)
            kh = kv_ref[:, h * hd:(h + 1) * hd].astype(bf16)
            vh = kv_ref[:, D + h * hd:D + (h + 1) * hd].astype(bf16)
            s = _nt(qh, kh)
            e = jnp.exp(s - jnp.max(s, axis=-1, keepdims=True))
            d = jnp.sum(e, axis=-1, keepdims=True)
            o_ref[:, h * hd:(h + 1) * hd] = _nn(e.astype(bf16), vh) * (1.0 / d)

    return pl.pallas_call(
        body, grid=(T // tq,),
        in_specs=[pl.BlockSpec((tq, D), lambda i: (i, 0)), pl.BlockSpec((M, 2 * D), lambda i: (i // tps, 0))],
        out_specs=pl.BlockSpec((tq, D), lambda i: (i, 0)), out_shape=jax.ShapeDtypeStruct((T, D), f32),
        compiler_params=_params(("parallel",), 40), name="mem_attn_prompt")(q, mkv)


def _mem_attn_sample(q, cache, layer, dec_seq):
    T, D = q.shape
    _, S, M, _, H, hd = cache.shape
    scale = hd ** -0.5
    spb = 8 // dec_seq
    assert 8 % dec_seq == 0 and S % spb == 0
    ncp = spb * 2 * H

    def body(q_ref, c_ref, o_ref, buf, sem):
        i = pl.program_id(0)
        n = pl.num_programs(0)

        def copies(step, slot):
            return [pltpu.make_async_copy(c_ref.at[layer, step * spb + s, :, kv, h, :], buf.at[slot, (s * 2 + kv) * H + h],
                                          sem.at[slot, (s * 2 + kv) * H + h])
                    for s in range(spb) for kv in range(2) for h in range(H)]

        @pl.when(i == 0)
        def _():
            for cp in copies(0, 0):
                cp.start()

        @pl.when(i + 1 < n)
        def _():
            for cp in copies(i + 1, (i + 1) % 2):
                cp.start()

        slot = i % 2
        for cp in copies(i, slot):
            cp.wait()
        for s in range(spb):
            rows = slice(s * dec_seq, (s + 1) * dec_seq)
            for h in range(H):
                qh = (q_ref[rows, h * hd:(h + 1) * hd] * scale).astype(bf16)
                kh = buf[slot, (s * 2) * H + h].astype(bf16)
                vh = buf[slot, (s * 2 + 1) * H + h].astype(bf16)
                sc = _nt(qh, kh)
                e = jnp.exp(sc - jnp.max(sc, axis=-1, keepdims=True))
                d = jnp.sum(e, axis=-1, keepdims=True)
                o_ref[rows, h * hd:(h + 1) * hd] = _nn(e.astype(bf16), vh) * (1.0 / d)

    return pl.pallas_call(
        body, grid=(S // spb,),
        in_specs=[pl.BlockSpec((8, D), lambda i: (i, 0)), pl.BlockSpec(memory_space=pl.ANY)],
        out_specs=pl.BlockSpec((8, D), lambda i: (i, 0)), out_shape=jax.ShapeDtypeStruct((T, D), f32),
        scratch_shapes=[pltpu.VMEM((2, ncp, M, hd), f32), pltpu.SemaphoreType.DMA((2, ncp))],
        compiler_params=_params(("arbitrary",), 40), name="mem_attn_sample")(q, cache)


def _cumsum_steps(n):
    k, out = 1, []
    while k < n:
        out.append(k)
        k *= 2
    return out


def _ssd_prompt(xbc, z, small, conv_w, conv_b, dt_bias, a_log, d_skip, ssm_norm, nb):
    T, C = xbc.shape
    d_ssm = z.shape[1]
    H = d_ssm // SSM_HEAD_DIM
    P, N, L = SSM_HEAD_DIM, SSM_STATE, SSD_CHUNK
    hpg = H // SSM_GROUPS
    kw = conv_w.shape[0]
    seq = T // nb
    nc = seq // L
    assert seq % L == 0 and kw - 1 <= 8

    pad = lambda v: jnp.zeros((1, LANES), f32).at[0, :H].set(v)
    col = lambda v: jnp.broadcast_to(v[:, None], (H, LANES))

    rep = (jnp.arange(LANES)[:, None] == jnp.arange(d_ssm)[None, :] // P).astype(bf16)

    def body(xbc_ref, z_ref, sm_ref, cw_ref, cb_ref, dtb_r, alog_r, dskx_ref, dtb_c, alog_c, nrm_ref, rep_ref,
             y_ref, st_ref, cv_ref, xpad, hst):
        c = pl.program_id(1)

        @pl.when(c == 0)
        def _():
            xpad[0:8] = jnp.zeros((8, C), f32)
            hst[...] = jnp.zeros_like(hst)

        xpad[8:8 + L] = xbc_ref[...]
        acc = cb_ref[...] + cw_ref[0:1] * xpad[9 - kw:9 - kw + L]
        for k in range(1, kw):
            acc += cw_ref[k:k + 1] * xpad[9 - kw + k:9 - kw + k + L]
        xc = jax.nn.silu(acc)
        xpad[0:8] = xpad[L:L + 8]

        @pl.when(c == nc - 1)
        def _():
            cv_ref[...] = xpad[9 - kw:8]

        sm = sm_ref[...]
        dt_c = jax.nn.softplus(sm + dtb_r[...])
        acs_c = dt_c * (-jnp.exp(alog_r[...]))
        row = lax.broadcasted_iota(jnp.int32, (L, LANES), 0)
        for k in _cumsum_steps(L):
            acs_c = acs_c + jnp.where(row >= k, pltpu.roll(acs_c, k, 0), 0.0)
        dt_r = jax.nn.softplus(sm.T[0:H] + dtb_c[...])
        acs_r = dt_r * (-jnp.exp(alog_c[...]))
        lane = lax.broadcasted_iota(jnp.int32, (H, L), 1)
        for k in _cumsum_steps(L):
            acs_r = acs_r + jnp.where(lane >= k, pltpu.roll(acs_r, k, 1), 0.0)

        def per_head_lanes(v):
            hi, lo = _split2(v)
            return _nn(hi, rep_ref[...]) + _nn(lo, rep_ref[...])

        alast_c = acs_c[L - 1:L, :]
        xs_all = xc[:, 0:d_ssm]
        xdt_all = xs_all * per_head_lanes(dt_c)
        xd_all = (xdt_all * per_head_lanes(jnp.exp(alast_c - acs_c))).astype(bf16)
        xdt_bf = xdt_all.astype(bf16)
        causal = lax.broadcasted_iota(jnp.int32, (L, L), 0) >= lax.broadcasted_iota(jnp.int32, (L, L), 1)
        ydiags, yoffs = [], []
        for g in range(SSM_GROUPS):
            bg = xc[:, d_ssm + g * N:d_ssm + (g + 1) * N].astype(bf16)
            cg = xc[:, d_ssm + (SSM_GROUPS + g) * N:d_ssm + (SSM_GROUPS + g + 1) * N].astype(bf16)
            cb = _nt(cg, bg)
            hg = hst[g * hpg:(g + 1) * hpg].reshape(hpg * P, N)
            yoffs.append(_nt(cg, hg.astype(bf16)))
            for c4 in range(hpg):
                h = g * hpg + c4
                dec = jnp.exp(jnp.where(causal, acs_c[:, h:h + 1] - acs_r[h:h + 1, :], NEG))
                ydiags.append(_nn((cb * dec).astype(bf16), xdt_bf[:, h * P:(h + 1) * P]))
            dh = _tn(xd_all[:, g * hpg * P:(g + 1) * hpg * P], bg)
            for c4 in range(hpg):
                h = g * hpg + c4
                hst[h] = hst[h] * jnp.exp(alast_c[:, h:h + 1]) + dh[c4 * P:(c4 + 1) * P]

        y = (jnp.concatenate(ydiags, axis=1) + jnp.concatenate(yoffs, axis=1) * per_head_lanes(jnp.exp(acs_c))
             + dskx_ref[...] * xs_all)
        y_ref[...] = _rms(y * jax.nn.silu(z_ref[...]), nrm_ref[...])

        @pl.when(c == nc - 1)
        def _():
            st_ref[...] = hst[...]

    tok = lambda w: pl.BlockSpec((L, w), lambda b, c: (b * nc + c, 0))
    cst = lambda shp: pl.BlockSpec(shp, lambda b, c: tuple(0 for _ in shp))
    return pl.pallas_call(
        body, grid=(nb, nc),
        in_specs=[tok(C), tok(d_ssm), tok(LANES), cst((kw, C)), cst((1, C)), cst((1, LANES)), cst((1, LANES)),
                  cst((1, d_ssm)), cst((H, LANES)), cst((H, LANES)), cst((1, d_ssm)), cst((LANES, d_ssm))],
        out_specs=[tok(d_ssm), pl.BlockSpec((None, H, P, N), lambda b, c: (b, 0, 0, 0)),
                   pl.BlockSpec((None, kw - 1, C), lambda b, c: (b, 0, 0))],
        out_shape=[jax.ShapeDtypeStruct((T, d_ssm), f32), jax.ShapeDtypeStruct((nb, H, P, N), f32),
                   jax.ShapeDtypeStruct((nb, kw - 1, C), f32)],
        scratch_shapes=[pltpu.VMEM((L + 8, C), f32), pltpu.VMEM((H, P, N), f32)],
        compiler_params=_params(("parallel", "arbitrary"), 32), name="ssd_prompt",
    )(xbc, z, small, conv_w, conv_b.reshape(1, C), pad(dt_bias), pad(a_log), jnp.repeat(d_skip, P).reshape(1, d_ssm),
      col(dt_bias), col(a_log), ssm_norm.reshape(1, d_ssm), rep)


def _ssd_sample(xbc, z, small, state, conv_prev, layer, conv_w, conv_b, dt_bias, a_log, d_skip, ssm_norm, dec_seq):
    T, C = xbc.shape
    S = T // dec_seq
    d_ssm = z.shape[1]
    H = d_ssm // SSM_HEAD_DIM
    P, N = SSM_HEAD_DIM, SSM_STATE
    hpg = H // SSM_GROUPS
    kw = conv_w.shape[0]
    Lt = dec_seq
    tmaj = lambda a: a.reshape(S, Lt, a.shape[1]).transpose(1, 0, 2)

    def conv_body(x_ref, prev_ref, sm_ref, cw_ref, cb_ref, xt_ref, smt_ref, cv_ref):
        rows = [prev_ref[k] for k in range(kw - 1)] + [x_ref[t] for t in range(Lt)]
        for t in range(Lt):
            acc = cb_ref[...] + cw_ref[0:1] * rows[t]
            for k in range(1, kw):
                acc += cw_ref[k:k + 1] * rows[t + k]
            xt_ref[t] = jax.nn.silu(acc).T
            smt_ref[t] = sm_ref[t].T
        for k in range(kw - 1):
            cv_ref[k] = rows[Lt + k]

    xt, smt, conv_new = pl.pallas_call(
        conv_body, grid=(1,),
        in_specs=[pl.BlockSpec((Lt, S, C), lambda i: (0, 0, 0)), pl.BlockSpec((None, kw - 1, S, C), lambda i: (layer, 0, 0, 0)),
                  pl.BlockSpec((Lt, S, LANES), lambda i: (0, 0, 0)), pl.BlockSpec((kw, C), lambda i: (0, 0)),
                  pl.BlockSpec((1, C), lambda i: (0, 0))],
        out_specs=[pl.BlockSpec((Lt, C, S), lambda i: (0, 0, 0)), pl.BlockSpec((Lt, LANES, S), lambda i: (0, 0, 0)),
                   pl.BlockSpec((kw - 1, S, C), lambda i: (0, 0, 0))],
        out_shape=[jax.ShapeDtypeStruct((Lt, C, S), f32), jax.ShapeDtypeStruct((Lt, LANES, S), f32),
                   jax.ShapeDtypeStruct((kw - 1, S, C), f32)],
        compiler_params=_params(("arbitrary",), 32), name="ssd_sample_conv",
    )(tmaj(xbc), conv_prev, tmaj(small), conv_w, conv_b.reshape(1, C))

    def state_body(st_ref, xt_ref, smt_ref, dtb_ref, alog_ref, dsk_ref, nst_ref, y_ref, xdt_s):
        h = pl.program_id(0)
        g = h // hpg
        xoff = pl.multiple_of(h * P, P)
        boff = pl.multiple_of(d_ssm + g * N, N)
        coff = pl.multiple_of(d_ssm + (SSM_GROUPS + g) * N, N)
        a = -jnp.exp(alog_ref[pl.ds(h, 1), :])
        dsk = dsk_ref[pl.ds(h, 1), :]
        decs, bs, cs = [], [], []
        for t in range(Lt):
            dt = jax.nn.softplus(smt_ref[t, pl.ds(h, 1), :] + dtb_ref[pl.ds(h, 1), :])
            x = xt_ref[t, pl.ds(xoff, P), :]
            decs.append(jnp.exp(dt * a))
            xdt_s[t] = x * dt
            bs.append(xt_ref[t, pl.ds(boff, N), :])
            cs.append(xt_ref[t, pl.ds(coff, N), :])
            y_ref[t] = dsk * x

        def step(p, carry):
            r0 = pl.multiple_of(p * N, N)
            hp = st_ref[pl.ds(r0, N), :]
            for t in range(Lt):
                hp = hp * decs[t] + xdt_s[t, pl.ds(p, 1), :] * bs[t]
                y_ref[t, pl.ds(p, 1), :] += jnp.sum(cs[t] * hp, axis=0, keepdims=True)
            nst_ref[pl.ds(r0, N), :] = hp
            return carry

        lax.fori_loop(0, P, step, 0)

    colS = lambda v: jnp.broadcast_to(v[:, None], (H, S))
    new_state, yt = pl.pallas_call(
        state_body, grid=(H,),
        in_specs=[pl.BlockSpec((None, None, P * N, S), lambda h: (layer, h, 0, 0)),
                  pl.BlockSpec((Lt, C, S), lambda h: (0, 0, 0)), pl.BlockSpec((Lt, LANES, S), lambda h: (0, 0, 0)),
                  pl.BlockSpec((H, S), lambda h: (0, 0)), pl.BlockSpec((H, S), lambda h: (0, 0)),
                  pl.BlockSpec((H, S), lambda h: (0, 0))],
        out_specs=[pl.BlockSpec((None, P * N, S), lambda h: (h, 0, 0)), pl.BlockSpec((Lt, P, S), lambda h: (0, h, 0))],
        out_shape=[jax.ShapeDtypeStruct((H, P * N, S), f32), jax.ShapeDtypeStruct((Lt, d_ssm, S), f32)],
        scratch_shapes=[pltpu.VMEM((Lt, P, S), f32)],
        compiler_params=_params(("parallel",), 32), name="ssd_sample_state",
    )(state, xt, smt, colS(dt_bias), colS(a_log), colS(d_skip))

    def out_body(yt_ref, z_ref, nrm_ref, o_ref):
        for t in range(Lt):
            y = yt_ref[t].T * jax.nn.silu(z_ref[t])
            o_ref[t] = _rms(y, nrm_ref[...])

    y = pl.pallas_call(
        out_body, grid=(1,),
        in_specs=[pl.BlockSpec((Lt, d_ssm, S), lambda i: (0, 0, 0)), pl.BlockSpec((Lt, S, d_ssm), lambda i: (0, 0, 0)),
                  pl.BlockSpec((1, d_ssm), lambda i: (0, 0))],
        out_specs=pl.BlockSpec((Lt, S, d_ssm), lambda i: (0, 0, 0)),
        out_shape=jax.ShapeDtypeStruct((Lt, S, d_ssm), f32),
        compiler_params=_params(("arbitrary",), 32), name="ssd_sample_out",
    )(yt, tmaj(z), ssm_norm.reshape(1, d_ssm))
    return y.transpose(1, 0, 2).reshape(T, d_ssm), new_state, conv_new


def _slot_block(slot):
    return jnp.where(slot < LANES // 2, 2 * slot, 2 * (slot - LANES // 2) + 1)


def _pool_consts(cmp_wk, cmp_wv, rows):
    r = jnp.arange(rows)
    tile = lambda w: jnp.tile(w.T[:, r % CMP_BLOCK], (N_KV, 1))
    wt = jnp.concatenate([tile(cmp_wk), tile(cmp_wv)], axis=0)
    blk = r // CMP_BLOCK
    slot = jnp.where(blk % 2 == 0, blk // 2, LANES // 2 + blk // 2)
    seg = (slot[:, None] == jnp.arange(LANES)[None, :]).astype(bf16)
    return wt, seg


def _expand_const(n_keys):
    blk = jnp.arange(n_keys) // SEL_BLOCK
    return (jnp.arange(LANES)[:, None] == blk[None, :]).astype(bf16)


def _cmp_pool_prompt(kvt, wt, seg):
    nb, _, L = kvt.shape
    F = 2 * N_KV * HEAD_DIM

    def body(kv_ref, wt_ref, seg_ref, segt_ref, pool_ref, poolt_ref):
        hi, lo = _split2(kv_ref[...] * wt_ref[...])
        poolt_ref[...] = _nn(hi, seg_ref[...]) + _nn(lo, seg_ref[...])
        pool_ref[...] = _nt(segt_ref[...], hi) + _nt(segt_ref[...], lo)

    return pl.pallas_call(
        body, grid=(nb,),
        in_specs=[pl.BlockSpec((None, F, L), lambda b: (b, 0, 0)), pl.BlockSpec((F, L), lambda b: (0, 0)),
                  pl.BlockSpec((L, LANES), lambda b: (0, 0)), pl.BlockSpec((LANES, L), lambda b: (0, 0))],
        out_specs=[pl.BlockSpec((None, LANES, F), lambda b: (b, 0, 0)), pl.BlockSpec((None, F, LANES), lambda b: (b, 0, 0))],
        out_shape=[jax.ShapeDtypeStruct((nb, LANES, F), f32), jax.ShapeDtypeStruct((nb, F, LANES), f32)],
        compiler_params=_params(("parallel",), 32), name="cmp_pool_prompt")(kvt, wt, seg, seg.T)


def _select_blocks(score, valid, blk, n_blocks, axis):
    cnt = jnp.zeros(score.shape, f32)
    for i in range(n_blocks):
        si = score[i:i + 1, :] if axis == 0 else score[:, i:i + 1]
        beat = (si > score) | ((si == score) & (blk > i))
        cnt += jnp.where(beat, 1.0, 0.0)
    return jnp.where((cnt < TOP_N) & valid, 1.0, 0.0)


def _nsa_prompt(q, small, kvt, wint, pool, poolt, expand, gate_b, goff, tq=128):
    T, dq = q.shape
    nb, _, L = kvt.shape
    hd = HEAD_DIM
    H = dq // hd
    hpg = H // N_KV
    F = 2 * N_KV * hd
    nq = L // tq
    scale = hd ** -0.5
    n_sel = L // SEL_BLOCK
    span = tq + WINDOW
    assert L % SEL_TK == 0 and L % tq == 0 and tq % SEL_BLOCK == 0 and n_sel <= LANES // 2 and WINDOW % LANES == 0

    def body(q_ref, sm_ref, kv_ref, win_ref, pool_ref, poolt_ref, e_ref, gb_ref, o_ref, kbf, wbf, bias_ref):
        qi = pl.program_id(1)

        @pl.when(qi == 0)
        def _():
            kbf[...] = kv_ref[...].astype(bf16)
            wbf[:, 0:WINDOW] = jnp.zeros((F, WINDOW), bf16)
            wbf[:, WINDOW:] = win_ref[...].astype(bf16)

        q0 = pl.multiple_of(qi * tq, tq)
        qs = q_ref[...] * scale
        gates = jax.nn.sigmoid(sm_ref[...] + gb_ref[...])
        qpos_c = q0 + lax.broadcasted_iota(jnp.int32, (tq, 1), 0)
        qpos_r = q0 + lax.broadcasted_iota(jnp.int32, (1, tq), 1)
        qpos_c4 = jnp.concatenate([qpos_c] * hpg, axis=0)
        qpos_r4 = jnp.concatenate([qpos_r] * hpg, axis=1)
        slot_r = _slot_block(lax.broadcasted_iota(jnp.int32, (1, LANES), 1))
        slot_c = _slot_block(lax.broadcasted_iota(jnp.int32, (LANES, 1), 0))
        outs = []
        for g in range(N_KV):
            qg = jnp.concatenate([qs[:, (g * hpg + c) * hd:(g * hpg + c + 1) * hd] for c in range(hpg)], axis=0).astype(bf16)
            kct = poolt_ref[g * hd:(g + 1) * hd, :].astype(bf16)
            kc = pool_ref[:, g * hd:(g + 1) * hd].astype(bf16)
            vc = pool_ref[:, (N_KV + g) * hd:(N_KV + g + 1) * hd].astype(bf16)
            e_cmp, r_cmp = _masked_exp(_nn(qg, kct), (slot_r * CMP_BLOCK + CMP_BLOCK - 1) <= qpos_c4)
            o_cmp = _nn(e_cmp.astype(bf16), vc)
            pt = _masked_softmax(_nt(kc, qg), (slot_c * CMP_BLOCK + CMP_BLOCK - 1) <= qpos_r4, axis=0)
            impt = pt[:, 0:tq]
            for c in range(1, hpg):
                impt = impt + pt[:, c * tq:(c + 1) * tq]
            imp = impt[0:n_sel] + impt[LANES // 2:LANES // 2 + n_sel]
            blk = lax.broadcasted_iota(jnp.int32, (n_sel, 1), 0)
            cur = qpos_r // SEL_BLOCK
            valid = blk <= cur
            forced = (blk == 0) | (blk == cur) | (blk == cur - 1)
            score = jnp.where(valid, jnp.where(forced, FORCE_SCORE, imp), NEG)
            selt = _select_blocks(score, valid, blk, n_sel, axis=0)
            sel = jnp.concatenate([selt, jnp.zeros((LANES - n_sel, tq), f32)], axis=0).T
            expd = _nn(sel.astype(bf16), e_ref[...])
            kpos = lax.broadcasted_iota(jnp.int32, (1, L), 1)
            bias_ref[...] = jnp.where((expd > 0.5) & (kpos <= qpos_c), 0.0, NEG)

            def sel_step(j, carry):
                m, l, acc = carry
                k0 = pl.multiple_of(j * SEL_TK, SEL_TK)
                kt = kbf[g * hd:(g + 1) * hd, pl.ds(k0, SEL_TK)]
                vt = kbf[(N_KV + g) * hd:(N_KV + g + 1) * hd, pl.ds(k0, SEL_TK)]
                s = _nn(qg, kt).reshape(hpg, tq, SEL_TK) + bias_ref[:, pl.ds(k0, SEL_TK)][None]
                s = s.reshape(hpg * tq, SEL_TK)
                m_new = jnp.maximum(m, jnp.max(s, axis=-1, keepdims=True))
                alpha = jnp.exp(m - m_new)
                pj = jnp.exp(s - m_new)
                l = alpha * l + jnp.sum(pj, axis=-1, keepdims=True)
                acc = alpha * acc + _nt(pj.astype(bf16), vt)
                return m_new, l, acc

            init = (jnp.full((hpg * tq, 1), NEG, f32), jnp.zeros((hpg * tq, 1), f32), jnp.zeros((hpg * tq, hd), f32))
            _, l_sel, o_sel = lax.fori_loop(0, (q0 + tq + SEL_TK - 1) // SEL_TK, sel_step, init)
            r_sel = 1.0 / l_sel
            kw_ = wbf[g * hd:(g + 1) * hd, pl.ds(q0, span)]
            vw_ = wbf[(N_KV + g) * hd:(N_KV + g + 1) * hd, pl.ds(q0, span)]
            col = lax.broadcasted_iota(jnp.int32, (1, span), 1)
            dist = (lax.broadcasted_iota(jnp.int32, (tq, 1), 0) + WINDOW) - col
            wmask = (dist >= 0) & (dist < WINDOW) & (col + q0 >= WINDOW)
            wmask4 = jnp.concatenate([wmask] * hpg, axis=0)
            e_win, r_win = _masked_exp(_nn(qg, kw_), wmask4)
            o_win = _nt(e_win.astype(bf16), vw_)
            for c in range(hpg):
                h = g * hpg + c
                rows = slice(c * tq, (c + 1) * tq)
                gc, gs, gw = (gates[:, goff + k * H + h:goff + k * H + h + 1] for k in range(3))
                outs.append((gc * r_cmp[rows]) * o_cmp[rows] + (gs * r_sel[rows]) * o_sel[rows] + (gw * r_win[rows]) * o_win[rows])
        o_ref[...] = jnp.concatenate(outs, axis=1)

    tok = lambda w: pl.BlockSpec((tq, w), lambda b, i: (b * nq + i, 0))
    return pl.pallas_call(
        body, grid=(nb, nq),
        in_specs=[tok(dq), tok(LANES), pl.BlockSpec((None, F, L), lambda b, i: (b, 1, 0)),
                  pl.BlockSpec((None, F, L), lambda b, i: (b, 0, 0)),
                  pl.BlockSpec((None, LANES, F), lambda b, i: (b, 0, 0)), pl.BlockSpec((None, F, LANES), lambda b, i: (b, 0, 0)),
                  pl.BlockSpec((LANES, L), lambda b, i: (0, 0)), pl.BlockSpec((1, LANES), lambda b, i: (0, 0))],
        out_specs=tok(dq), out_shape=jax.ShapeDtypeStruct((T, dq), f32),
        scratch_shapes=[pltpu.VMEM((F, L), bf16), pltpu.VMEM((F, L + WINDOW), bf16), pltpu.VMEM((tq, L), f32)],
        compiler_params=_params(("parallel", "arbitrary"), 48), name="nsa_prompt",
    )(q, small, kvt, wint, pool, poolt, expand, gate_b)


def _nsa_sample(q, kv_new, win_new, small, cache_t, win_t, page_table, layer, wt_page, seg, expand, gate_b, goff, dec_seq):
    T, dq = q.shape
    S, n_pages = page_table.shape
    page = cache_t.shape[-1]
    keep = win_t.shape[-1]
    wd = win_t.shape[2]
    past = n_pages * page
    hd = HEAD_DIM
    H = dq // hd
    hpg = H // N_KV
    F = 2 * N_KV * hd
    G2 = N_KV * hd
    Lt = dec_seq
    SB = SAMPLE_SEQS
    RT = SB * Lt
    RS = hpg * N_KV * Lt
    R = SB * RS
    scale = hd ** -0.5
    n_sel = past // SEL_BLOCK + 1
    new_blk = past // SEL_BLOCK
    k_start = past - keep
    assert N_KV == 2 and G2 == LANES and R == LANES and RT % 8 == 0 and S % SB == 0
    assert past % SEL_BLOCK == 0 and Lt <= SEL_BLOCK and n_sel <= LANES // 2
    assert page == LANES and keep % LANES == 0 and keep <= WINDOW and Lt <= LANES and wd == 2 * G2

    r_ = jnp.arange(R)
    seq_r, c_r, g_r, t_r = r_ // RS, (r_ // (N_KV * Lt)) % hpg, (r_ // Lt) % N_KV, r_ % Lt
    z_of_r = ((g_r * hpg + c_r) * SB + seq_r) * Lt + t_r
    perm = (z_of_r[:, None] == jnp.arange(R)[None, :]).astype(bf16)
    rep = jnp.stack([(jnp.arange(LANES)[:, None] == goff + k * H + jnp.arange(dq)[None, :] // hd) for k in range(3)]).astype(bf16)
    shift = (jnp.arange(Lt)[:, None] + (LANES - Lt) == jnp.arange(LANES)[None, :]).astype(bf16)

    def body(pt_ref, q_ref, kvn_ref, wn_ref, sm_ref, *refs):
        pages = [refs[sq * n_pages:(sq + 1) * n_pages] for sq in range(SB)]
        win_ref, wt_ref, seg_ref, e_ref, gb_ref, shift_ref, perm_ref, rep_ref, o_ref, wout_ref = refs[SB * n_pages:]
        qs = q_ref[...] * scale
        kvn = kvn_ref[...]
        wn = wn_ref[...]
        gates = jax.nn.sigmoid(sm_ref[...] + gb_ref[...])
        lane = lax.broadcasted_iota(jnp.int32, (1, LANES), 1)
        rowi = lax.broadcasted_iota(jnp.int32, (R, 1), 0)
        g_row = (rowi // Lt) % N_KV
        t_row = rowi % Lt
        qpos = past + t_row

        staged = []
        for g in range(N_KV):
            for c in range(hpg):
                pair = jnp.concatenate([qs[:, c * hd:(c + 1) * hd], qs[:, (hpg + c) * hd:(hpg + c + 1) * hd]], axis=1)
                staged.append(jnp.where((lane >= g * hd) & (lane < (g + 1) * hd), pair, 0.0))
        qbd32 = _nn(perm_ref[...], jnp.concatenate(staged, axis=0).astype(bf16))
        qbd = qbd32.astype(bf16)
        seq_rows = lambda a, sq: a[sq * RS:(sq + 1) * RS]
        per_seq = lambda fn: jnp.concatenate([fn(sq) for sq in range(SB)], axis=0)
        own_lanes = lambda o: jnp.where(g_row == 0, o, pltpu.roll(o, hd, 1))[:, 0:hd]

        poolt = []
        for sq in range(SB):
            hi, lo = _split2(jnp.concatenate([pages[sq][p][0:F, :] * wt_ref[...] for p in range(n_pages)], axis=1))
            poolt.append(_nn(hi, seg_ref[...]) + _nn(lo, seg_ref[...]))

        s_c = per_seq(lambda sq: _nn(seq_rows(qbd, sq), poolt[sq][0:G2].astype(bf16)))
        p = _masked_softmax(s_c, (_slot_block(lane) * CMP_BLOCK + CMP_BLOCK - 1) <= qpos)
        p_bf = p.astype(bf16)
        o_cmp = own_lanes(per_seq(lambda sq: _nt(seq_rows(p_bf, sq), poolt[sq][G2:2 * G2].astype(bf16))))

        def imp_seq(sq):
            ps = seq_rows(p, sq)
            acc = ps[0:N_KV * Lt]
            for c in range(1, hpg):
                acc = acc + ps[c * N_KV * Lt:(c + 1) * N_KV * Lt]
            return acc
        imp = per_seq(imp_seq)
        imp = imp + pltpu.roll(imp, LANES // 2, 1)
        nrow = SB * N_KV * Lt
        qpos_i = past + lax.broadcasted_iota(jnp.int32, (nrow, 1), 0) % Lt
        cur = qpos_i // SEL_BLOCK
        valid = (lane <= cur) & (lane < n_sel)
        forced = (lane == 0) | (lane == cur) | (lane == cur - 1)
        score = jnp.where(valid, jnp.where(forced, FORCE_SCORE, imp), NEG)
        sel = _select_blocks(score, valid, lane, n_sel, axis=1)
        expd = _nn(sel.astype(bf16), e_ref[...])
        kpos = lax.broadcasted_iota(jnp.int32, (1, past), 1)
        bias = jnp.where((expd > 0.5) & (kpos <= qpos_i), 0.0, NEG)
        gt = N_KV * Lt
        tile_c = lambda a: per_seq(lambda sq: jnp.concatenate([a[sq * gt:(sq + 1) * gt]] * hpg, axis=0))
        sel_new = tile_c(sel[:, new_blk:new_blk + 1]) > 0.5

        def new_rows(src, lo_lane):
            return [per_seq(lambda sq: jnp.broadcast_to(src[sq * Lt + t:sq * Lt + t + 1, lo_lane:lo_lane + G2], (RS, G2)))
                    for t in range(Lt)]

        def attend(s_past, k_new, v_new, ok_new, v_past_fn):
            s_new = [jnp.where(ok_new & (t_row >= t), jnp.sum(qbd32 * k_new[t], axis=-1, keepdims=True), NEG) for t in range(Lt)]
            m = jnp.max(s_past, axis=-1, keepdims=True)
            for s_t in s_new:
                m = jnp.maximum(m, s_t)
            e_past = jnp.exp(s_past - m)
            den = jnp.sum(e_past, axis=-1, keepdims=True)
            e_bf = e_past.astype(bf16)
            acc = per_seq(lambda sq: v_past_fn(sq, seq_rows(e_bf, sq)))
            for t, s_t in enumerate(s_new):
                e_t = jnp.exp(s_t - m)
                den = den + e_t
                acc = acc + e_t * v_new[t]
            return own_lanes(acc) * (1.0 / den)

        s_sel = per_seq(lambda sq: jnp.concatenate([_nn(seq_rows(qbd, sq), pages[sq][p][F:F + G2, :].astype(bf16))
                                                    for p in range(n_pages)], axis=1)) + tile_c(bias)

        def v_sel(sq, e):
            acc = _nt(e[:, 0:page], pages[sq][0][F + G2:F + 2 * G2, :].astype(bf16))
            for p in range(1, n_pages):
                acc += _nt(e[:, p * page:(p + 1) * page], pages[sq][p][F + G2:F + 2 * G2, :].astype(bf16))
            return acc

        o_sel = attend(s_sel, new_rows(kvn, F), new_rows(kvn, F + G2), sel_new, v_sel)
        dist = qpos - (k_start + lax.broadcasted_iota(jnp.int32, (1, keep), 1))
        s_win = jnp.where((dist >= 0) & (dist < WINDOW),
                          per_seq(lambda sq: _nn(seq_rows(qbd, sq), win_ref[sq, 0:G2, :].astype(bf16))), NEG)
        o_win = attend(s_win, new_rows(wn, 0), new_rows(wn, G2), jnp.full((R, 1), True),
                       lambda sq, e: _nt(e, win_ref[sq, G2:2 * G2, :].astype(bf16)))

        def token_major(o):
            hi, lo = _split2(o)
            st = _tn(perm_ref[...], hi) + _tn(perm_ref[...], lo)
            return jnp.concatenate([st[k * RT:(k + 1) * RT] for k in range(N_KV * hpg)], axis=1)

        def gate(k):
            hi, lo = _split2(gates)
            return _nn(hi, rep_ref[k]) + _nn(lo, rep_ref[k])

        o_ref[...] = gate(0) * token_major(o_cmp) + gate(1) * token_major(o_sel) + gate(2) * token_major(o_win)

        for sq in range(SB):
            rolled = pltpu.roll(win_ref[sq], keep - Lt, 1)
            hi, mid, lo = _split3(wn[sq * Lt:(sq + 1) * Lt])
            new_t = _tn(hi, shift_ref[...]) + _tn(mid, shift_ref[...]) + _tn(lo, shift_ref[...])
            last = jnp.where(lane >= LANES - Lt, new_t, rolled[:, keep - LANES:keep])
            if keep > LANES:
                wout_ref[sq, :, 0:keep - LANES] = rolled[:, 0:keep - LANES]
            wout_ref[sq, :, keep - LANES:keep] = last

    tok = lambda w: pl.BlockSpec((RT, w), lambda i, pt: (i, 0))
    cst = lambda shp: pl.BlockSpec(shp, lambda i, pt: tuple(0 for _ in shp))
    page_spec = lambda sq, p: pl.BlockSpec((None, None, 2 * F, page), lambda i, pt: (layer, pt[i * SB + sq, p], 0, 0))
    grid_spec = pltpu.PrefetchScalarGridSpec(
        num_scalar_prefetch=1, grid=(S // SB,),
        in_specs=[tok(dq), tok(2 * F), tok(wd), tok(LANES)] + [page_spec(sq, p) for sq in range(SB) for p in range(n_pages)] + [
            pl.BlockSpec((None, SB, wd, keep), lambda i, pt: (layer, i, 0, 0)),
            cst((F, page)), cst((past, LANES)), cst((LANES, past)), cst((1, LANES)), cst((Lt, LANES)), cst((R, R)),
            cst((3, LANES, dq))],
        out_specs=[tok(dq), pl.BlockSpec((SB, wd, keep), lambda i, pt: (i, 0, 0))])
    return pl.pallas_call(
        body, grid_spec=grid_spec,
        out_shape=[jax.ShapeDtypeStruct((T, dq), f32), jax.ShapeDtypeStruct((S, wd, keep), f32)],
        compiler_params=_params(("parallel",), 56), name="nsa_sample",
    )(page_table, q, kv_new, win_new, small, *([cache_t] * (SB * n_pages)), win_t, wt_page, seg, expand, gate_b, shift, perm, rep)


def kernel(x_prompt, x_sample, mem_prompt, cache_nsa_kv, cache_mem_kv, state_nsa_win, state_ssm, state_conv, page_table, norm_mix_pre, w_in, conv_w, conv_b, dt_bias, a_log, d_skip, ssm_norm, cmp_wk, cmp_wv, gate_b, w_out, norm_mix_post, norm_mem_src, w_mem_q, w_mem_k, w_mem_v, w_mem_o, norm_mem_pre, norm_mem_post, norm_ffn_pre, w_up, w_down, norm_ffn_post):
    B, L, D = x_prompt.shape
    S, Lt, _ = x_sample.shape
    depth = w_in.shape[0]
    M = mem_prompt.shape[1]
    d_ssm = ssm_norm.shape[1]
    C = conv_w.shape[2]
    H = dt_bias.shape[1]
    P, N = state_ssm.shape[3], state_ssm.shape[4]
    d_att = w_out.shape[1] - d_ssm
    Ha = d_att // HEAD_DIM
    n_pool, page = cache_nsa_kv.shape[1], cache_nsa_kv.shape[2]
    keep = state_nsa_win.shape[2]
    kvw = 4 * N_KV * HEAD_DIM
    winw = 2 * N_KV * HEAD_DIM
    past = page_table.shape[1] * page
    assert H + 3 * Ha <= LANES

    cache_t = cache_nsa_kv.transpose(0, 1, 3, 4, 5, 2).reshape(depth, n_pool, kvw, page)
    win_t = state_nsa_win.transpose(0, 1, 3, 4, 5, 2).reshape(depth, S, winw, keep)
    ssm_t = state_ssm.transpose(0, 2, 3, 4, 1).reshape(depth, H, P * N, S)
    conv_t = state_conv.transpose(0, 2, 1, 3)

    o1 = d_ssm
    o2 = o1 + C
    o3 = o2 + H
    o4 = o3 + d_att
    o5 = o4 + kvw
    o6 = o5 + winw
    n_gate = 3 * Ha

    expand_p = _expand_const(L)
    expand_s = _expand_const(past)
    xp = x_prompt.reshape(B * L, D)
    xs = x_sample.reshape(S * Lt, D)
    mem = mem_prompt.reshape(B * M, D)
    kv_p, kv_s, win_p, win_s, ssm_p, ssm_s, conv_p, conv_s, mkv_out = [], [], [], [], [], [], [], [], []
    for l in range(depth):
        wl = w_in[l].astype(bf16)
        w_z, w_xbc, w_q = wl[:, :o1], wl[:, o1:o2], wl[:, o3:o4]
        w_kv, w_win = wl[:, o4:o5], wl[:, o5:o6]
        w_small = jnp.concatenate([wl[:, o2:o3], wl[:, o6:o6 + n_gate], jnp.zeros((D, LANES - H - n_gate), bf16)], axis=1)
        gb = jnp.zeros((1, LANES), f32).at[0, H:H + n_gate].set(gate_b[l])
        wo = w_out[l].astype(bf16)
        wt_full, seg_full = _pool_consts(cmp_wk[l], cmp_wv[l], L)
        wt_page, seg_page = _pool_consts(cmp_wk[l], cmp_wv[l], past)

        mkv, = _norm_proj(mem, norm_mem_src[l], [jnp.concatenate([w_mem_k[l], w_mem_v[l]], axis=1).astype(bf16)])
        mkv_out.append(mkv.reshape(B, M, 2, MEM_HEADS, D // MEM_HEADS))

        z, xbc, q, small, kvt, wint = _norm_proj(xp, norm_mix_pre[l], [w_z, w_xbc, w_q, w_small], [w_kv.T, w_win.T], seq=L)
        y_ssd, st_new, cv_new = _ssd_prompt(xbc, z, small, conv_w[l], conv_b[l], dt_bias[l], a_log[l], d_skip[l], ssm_norm[l], B)
        pool, poolt = _cmp_pool_prompt(kvt, wt_full, seg_full)
        o_att = _nsa_prompt(q, small, kvt, wint, pool, poolt, expand_p, gb, H)
        xp = _linear_post(xp, [y_ssd, o_att], [wo[:d_ssm], wo[d_ssm:]], norm_mix_post[l])
        kv_p.append(kvt)
        win_p.append(wint[:, :, L - min(WINDOW, L):])
        ssm_p.append(st_new)
        conv_p.append(cv_new)

        z, xbc, q, small, kvr, winr = _norm_proj(xs, norm_mix_pre[l], [w_z, w_xbc, w_q, w_small, w_kv, w_win])
        y_ssd, st_new, cv_new = _ssd_sample(xbc, z, small, ssm_t, conv_t, l, conv_w[l], conv_b[l], dt_bias[l], a_log[l],
                                            d_skip[l], ssm_norm[l], Lt)
        o_att, wnew = _nsa_sample(q, kvr, winr, small, cache_t, win_t, page_table, l, wt_page[:, :page], seg_page,
                                  expand_s, gb, H, Lt)
        xs = _linear_post(xs, [y_ssd, o_att], [wo[:d_ssm], wo[d_ssm:]], norm_mix_post[l])
        kv_s.append(kvr)
        win_s.append(wnew)
        ssm_s.append(st_new)
        conv_s.append(cv_new)

        wq, wmo = w_mem_q[l].astype(bf16), w_mem_o[l].astype(bf16)
        qm, = _norm_proj(xp, norm_mem_pre[l], [wq])
        xp = _linear_post(xp, [_mem_attn_prompt(qm, mkv, L)], [wmo], norm_mem_post[l])
        qm, = _norm_proj(xs, norm_mem_pre[l], [wq])
        xs = _linear_post(xs, [_mem_attn_sample(qm, cache_mem_kv, l, Lt)], [wmo], norm_mem_post[l])

        wu, wd = w_up[l].astype(bf16), w_down[l].astype(bf16)
        xp = _ffn(xp, norm_ffn_pre[l], wu, wd, norm_ffn_post[l], tm=1024)
        xs = _ffn(xs, norm_ffn_pre[l], wu, wd, norm_ffn_post[l], tm=512)

    kv5 = (4, N_KV, HEAD_DIM)
    w5 = (2, N_KV, HEAD_DIM)
    return (xp.reshape(B, L, D), xs.reshape(S, Lt, D),
            jnp.stack(kv_p).reshape(depth, B, *kv5, L).transpose(0, 1, 5, 2, 3, 4),
            jnp.stack(kv_s).reshape(depth, S, Lt, *kv5),
            jnp.stack(win_p).reshape(depth, B, *w5, min(WINDOW, L)).transpose(0, 1, 5, 2, 3, 4),
            jnp.stack(win_s).reshape(depth, S, *w5, keep).transpose(0, 1, 5, 2, 3, 4),
            jnp.stack(ssm_p), jnp.stack(ssm_s).reshape(depth, H, P, N, S).transpose(0, 4, 1, 2, 3),
            jnp.stack(conv_p), jnp.stack(conv_s).transpose(0, 2, 1, 3),
            jnp.stack(mkv_out))
```

```python
import functools

import jax
import jax.numpy as jnp
from jax import lax
from jax.experimental import pallas as pl
from jax.experimental.pallas import tpu as pltpu

f32 = jnp.float32
bf16 = jnp.bfloat16
NEG = float("-inf")

SSM_HEAD_DIM = 64
SSM_GROUPS = 2
SSM_STATE = 64
SSD_CHUNK = 128
HEAD_DIM = 64
N_KV = 2
CMP_BLOCK = 32
SEL_BLOCK = 64
TOP_N = 16
WINDOW = 512
FORCE_SCORE = 1.0e4
MEM_HEADS = 4
EPS = 1e-6

LANES = 128
V7X_VMEM_BYTES = 64 * 1024 * 1024
SEL_TK = 512
SAMPLE_SEQS = 4
ROW_CHUNK = 256


def _params(sem, vmem_mb):
    assert vmem_mb * 1024 * 1024 < V7X_VMEM_BYTES
    return pltpu.CompilerParams(dimension_semantics=sem, vmem_limit_bytes=vmem_mb * 1024 * 1024)


def _nn(a, b):
    return jnp.dot(a, b, preferred_element_type=f32)


def _nt(a, b):
    return lax.dot_general(a, b, (((1,), (1,)), ((), ())), preferred_element_type=f32)


def _tn(a, b):
    return lax.dot_general(a, b, (((0,), (0,)), ((), ())), preferred_element_type=f32)


def _rms(x, g):
    return x * lax.rsqrt(jnp.mean(x * x, axis=-1, keepdims=True) + EPS) * g


def _masked_exp(s, mask, axis=-1):
    s = jnp.where(mask, s, NEG)
    m = jnp.max(s, axis=axis, keepdims=True)
    m = jnp.where(jnp.isfinite(m), m, 0.0)
    e = jnp.exp(s - m)
    d = jnp.sum(e, axis=axis, keepdims=True)
    return e, 1.0 / jnp.where(d > 0, d, 1.0)


def _masked_softmax(s, mask, axis=-1):
    e, r = _masked_exp(s, mask, axis)
    return e * r


def _split2(x):
    hi = x.astype(bf16)
    lo = (x - hi.astype(f32)).astype(bf16)
    return hi, lo


def _split3(x):
    hi = x.astype(bf16)
    r = x - hi.astype(f32)
    mid = r.astype(bf16)
    lo = (r - mid.astype(f32)).astype(bf16)
    return hi, mid, lo


def _norm_proj(x, g, ws, wts=(), seq=None, tm=512, row_dtypes=None):
    T, D = x.shape
    tm = min(tm, T)
    assert T % tm == 0
    ws, wts = list(ws), list(wts)
    n_row = len(ws)
    tps = (seq // tm) if wts else 1

    def body(x_ref, g_ref, *refs):
        w_refs, o_refs = refs[:len(ws) + len(wts)], refs[len(ws) + len(wts):]
        for r0 in range(0, tm, ROW_CHUNK):
            rows = slice(r0, min(r0 + ROW_CHUNK, tm))
            xn = _rms(x_ref[rows, :], g_ref[...]).astype(bf16)
            for k in range(n_row):
                o_refs[k][rows, :] = _nn(xn, w_refs[k][...]).astype(o_refs[k].dtype)
            for k in range(n_row, len(w_refs)):
                o_refs[k][:, rows] = _nt(w_refs[k][...], xn)

    in_specs = [pl.BlockSpec((tm, D), lambda i: (i, 0)), pl.BlockSpec((1, D), lambda i: (0, 0))]
    in_specs += [pl.BlockSpec(w.shape, lambda i: (0, 0)) for w in ws + wts]
    out_specs = [pl.BlockSpec((tm, w.shape[1]), lambda i: (i, 0)) for w in ws]
    out_specs += [pl.BlockSpec((None, w.shape[0], tm), lambda i: (i // tps, 0, i % tps)) for w in wts]
    row_dtypes = row_dtypes or [f32] * n_row
    out_shape = [jax.ShapeDtypeStruct((T, w.shape[1]), dt) for w, dt in zip(ws, row_dtypes)]
    out_shape += [jax.ShapeDtypeStruct((T // seq, w.shape[0], seq), f32) for w in wts]
    return pl.pallas_call(body, grid=(T // tm,), in_specs=in_specs, out_specs=out_specs, out_shape=out_shape,
                          compiler_params=_params(("parallel",), 48), name="norm_proj")(x, g.reshape(1, D), *ws, *wts)


def _linear_post(res, acts, ws, g, tm=512):
    T, D = res.shape
    tm = min(tm, T)
    n = len(acts)

    def body(*refs):
        a_refs, w_refs = refs[:n], refs[n:2 * n]
        g_ref, res_ref, o_ref = refs[2 * n:]
        for r0 in range(0, tm, ROW_CHUNK):
            rows = slice(r0, min(r0 + ROW_CHUNK, tm))
            acc = _nn(a_refs[0][rows, :].astype(bf16), w_refs[0][...])
            for k in range(1, n):
                acc += _nn(a_refs[k][rows, :].astype(bf16), w_refs[k][...])
            o_ref[rows, :] = res_ref[rows, :] + _rms(acc, g_ref[...])

    in_specs = [pl.BlockSpec((tm, a.shape[1]), lambda i: (i, 0)) for a in acts]
    in_specs += [pl.BlockSpec(w.shape, lambda i: (0, 0)) for w in ws]
    in_specs += [pl.BlockSpec((1, D), lambda i: (0, 0)), pl.BlockSpec((tm, D), lambda i: (i, 0))]
    return pl.pallas_call(body, grid=(T // tm,), in_specs=in_specs, out_specs=pl.BlockSpec((tm, D), lambda i: (i, 0)),
                          out_shape=jax.ShapeDtypeStruct((T, D), f32),
                          compiler_params=_params(("parallel",), 40), name="linear_post")(*acts, *ws, g.reshape(1, D), res)


def _ffn(x, g1, wu, wd, g2, tm, tc=1024):
    T, D = x.shape
    F = wu.shape[1]
    tm = min(tm, T)
    nj = F // tc

    def body(x_ref, g1_ref, wu_ref, wd_ref, g2_ref, o_ref, xn_s, acc_s):
        j = pl.program_id(1)

        @pl.when(j == 0)
        def _():
            xn_s[...] = _rms(x_ref[...], g1_ref[...]).astype(bf16)
            acc_s[...] = jnp.zeros_like(acc_s)

        for r0 in range(0, tm, ROW_CHUNK):
            rows = slice(r0, min(r0 + ROW_CHUNK, tm))
            h = _nn(xn_s[rows, :], wu_ref[...])
            h = jnp.square(jnp.maximum(h, 0.0))
            acc_s[rows, :] += _nn(h.astype(bf16), wd_ref[...])

        @pl.when(j == nj - 1)
        def _():
            o_ref[...] = x_ref[...] + _rms(acc_s[...], g2_ref[...])

    return pl.pallas_call(
        body, grid=(T // tm, nj),
        in_specs=[pl.BlockSpec((tm, D), lambda i, j: (i, 0)), pl.BlockSpec((1, D), lambda i, j: (0, 0)),
                  pl.BlockSpec((D, tc), lambda i, j: (0, j)), pl.BlockSpec((tc, D), lambda i, j: (j, 0)),
                  pl.BlockSpec((1, D), lambda i, j: (0, 0))],
        out_specs=pl.BlockSpec((tm, D), lambda i, j: (i, 0)),
        out_shape=jax.ShapeDtypeStruct((T, D), f32),
        scratch_shapes=[pltpu.VMEM((tm, D), bf16), pltpu.VMEM((tm, D), f32)],
        compiler_params=_params(("parallel", "arbitrary"), 48), name="ffn")(x, g1.reshape(1, D), wu, wd, g2.reshape(1, D))


def _mem_attn_prompt(q, mkv, seq, tq=512):
    T, D = q.shape
    M = mkv.shape[0] // (T // seq)
    hd = D // MEM_HEADS
    scale = hd ** -0.5
    tps = seq // tq

    def body(q_ref, kv_ref, o_ref):
        for h in range(MEM_HEADS):
            qh = (q_ref[:, h * hd:(h + 1) * hd] * scale).astype(bf16)
            kh = kv_ref[:, h * hd:(h + 1) * hd].astype(bf16)
            vh = kv_ref[:, D + h * hd:D + (h + 1) * hd].astype(bf16)
            s = _nt(qh, kh)
            e = jnp.exp(s - jnp.max(s, axis=-1, keepdims=True))
            d = jnp.sum(e, axis=-1, keepdims=True)
            o_ref[:, h * hd:(h + 1) * hd] = (_nn(e.astype(bf16), vh) * (1.0 / d)).astype(o_ref.dtype)

    return pl.pallas_call(
        body, grid=(T // tq,),
        in_specs=[pl.BlockSpec((tq, D), lambda i: (i, 0)), pl.BlockSpec((M, 2 * D), lambda i: (i // tps, 0))],
        out_specs=pl.BlockSpec((tq, D), lambda i: (i, 0)), out_shape=jax.ShapeDtypeStruct((T, D), bf16),
        compiler_params=_params(("parallel",), 40), name="mem_attn_prompt")(q, mkv)


def _mem_attn_sample(q, cache, layer, dec_seq):
    T, D = q.shape
    _, S, M, _, H, hd = cache.shape
    scale = hd ** -0.5
    spb = 8 // dec_seq
    assert 8 % dec_seq == 0 and S % spb == 0
    ncp = spb * 2 * H

    def body(q_ref, c_ref, o_ref, buf, sem):
        i = pl.program_id(0)
        n = pl.num_programs(0)

        def copies(step, slot):
            return [pltpu.make_async_copy(c_ref.at[layer, step * spb + s, :, kv, h, :], buf.at[slot, (s * 2 + kv) * H + h],
                                          sem.at[slot, (s * 2 + kv) * H + h])
                    for s in range(spb) for kv in range(2) for h in range(H)]

        @pl.when(i == 0)
        def _():
            for cp in copies(0, 0):
                cp.start()

        @pl.when(i + 1 < n)
        def _():
            for cp in copies(i + 1, (i + 1) % 2):
                cp.start()

        slot = i % 2
        for cp in copies(i, slot):
            cp.wait()
        pairs = [(s, h) for s in range(spb) for h in range(H)]
        qsc = (q_ref[...] * scale).astype(bf16)
        sc = jnp.concatenate([_nt(qsc[s * dec_seq:(s + 1) * dec_seq, h * hd:(h + 1) * hd], buf[slot, (s * 2) * H + h].astype(bf16))
                              for s, h in pairs], axis=0)
        e = jnp.exp(sc - jnp.max(sc, axis=-1, keepdims=True))
        r = 1.0 / jnp.sum(e, axis=-1, keepdims=True)
        for k, (s, h) in enumerate(pairs):
            rows = slice(k * dec_seq, (k + 1) * dec_seq)
            o_ref[s * dec_seq:(s + 1) * dec_seq, h * hd:(h + 1) * hd] = (
                _nn(e[rows].astype(bf16), buf[slot, (s * 2 + 1) * H + h].astype(bf16)) * r[rows])

    return pl.pallas_call(
        body, grid=(S // spb,),
        in_specs=[pl.BlockSpec((8, D), lambda i: (i, 0)), pl.BlockSpec(memory_space=pl.ANY)],
        out_specs=pl.BlockSpec((8, D), lambda i: (i, 0)), out_shape=jax.ShapeDtypeStruct((T, D), f32),
        scratch_shapes=[pltpu.VMEM((2, ncp, M, hd), f32), pltpu.SemaphoreType.DMA((2, ncp))],
        compiler_params=_params(("arbitrary",), 40), name="mem_attn_sample")(q, cache)


def _cumsum_steps(n):
    k, out = 1, []
    while k < n:
        out.append(k)
        k *= 2
    return out


def _ssd_prompt(xbc, z, small, conv_w, conv_b, dt_bias, a_log, d_skip, ssm_norm, nb):
    T, C = xbc.shape
    d_ssm = z.shape[1]
    H = d_ssm // SSM_HEAD_DIM
    P, N, L = SSM_HEAD_DIM, SSM_STATE, SSD_CHUNK
    hpg = H // SSM_GROUPS
    kw = conv_w.shape[0]
    seq = T // nb
    nc = seq // L
    assert seq % L == 0 and kw - 1 <= 8

    pad = lambda v: jnp.zeros((1, LANES), f32).at[0, :H].set(v)
    col = lambda v: jnp.broadcast_to(v[:, None], (H, LANES))

    rep = (jnp.arange(LANES)[:, None] == jnp.arange(d_ssm)[None, :] // P).astype(bf16)

    def body(xbc_ref, z_ref, sm_ref, cw_ref, cb_ref, dtb_r, alog_r, dskx_ref, dtb_c, alog_c, nrm_ref, rep_ref,
             y_ref, st_ref, cv_ref, xpad, hst):
        c = pl.program_id(1)

        @pl.when(c == 0)
        def _():
            xpad[0:8] = jnp.zeros((8, C), f32)
            hst[...] = jnp.zeros_like(hst)

        xpad[8:8 + L] = xbc_ref[...]
        acc = cb_ref[...] + cw_ref[0:1] * xpad[9 - kw:9 - kw + L]
        for k in range(1, kw):
            acc += cw_ref[k:k + 1] * xpad[9 - kw + k:9 - kw + k + L]
        xc = jax.nn.silu(acc)
        xpad[0:8] = xpad[L:L + 8]

        @pl.when(c == nc - 1)
        def _():
            cv_ref[...] = xpad[9 - kw:8]

        sm = sm_ref[...]
        dt_c = jax.nn.softplus(sm + dtb_r[...])
        acs_c = dt_c * (-jnp.exp(alog_r[...]))
        row = lax.broadcasted_iota(jnp.int32, (L, LANES), 0)
        for k in _cumsum_steps(L):
            acs_c = acs_c + jnp.where(row >= k, pltpu.roll(acs_c, k, 0), 0.0)
        dt_r = jax.nn.softplus(sm.T[0:H] + dtb_c[...])
        acs_r = dt_r * (-jnp.exp(alog_c[...]))
        lane = lax.broadcasted_iota(jnp.int32, (H, L), 1)
        for k in _cumsum_steps(L):
            acs_r = acs_r + jnp.where(lane >= k, pltpu.roll(acs_r, k, 1), 0.0)

        def per_head_lanes(v):
            hi, lo = _split2(v)
            return _nn(hi, rep_ref[...]) + _nn(lo, rep_ref[...])

        alast_c = acs_c[L - 1:L, :]
        xs_all = xc[:, 0:d_ssm]
        xdt_all = xs_all * per_head_lanes(dt_c)
        xd_all = (xdt_all * per_head_lanes(jnp.exp(alast_c - acs_c))).astype(bf16)
        xdt_bf = xdt_all.astype(bf16)
        causal = lax.broadcasted_iota(jnp.int32, (L, L), 0) >= lax.broadcasted_iota(jnp.int32, (L, L), 1)
        ydiags, yoffs = [], []
        for g in range(SSM_GROUPS):
            bg = xc[:, d_ssm + g * N:d_ssm + (g + 1) * N].astype(bf16)
            cg = xc[:, d_ssm + (SSM_GROUPS + g) * N:d_ssm + (SSM_GROUPS + g + 1) * N].astype(bf16)
            cb = _nt(cg, bg)
            hg = hst[g * hpg:(g + 1) * hpg].reshape(hpg * P, N)
            yoffs.append(_nt(cg, hg.astype(bf16)))
            for c4 in range(hpg):
                h = g * hpg + c4
                dec = jnp.exp(jnp.where(causal, acs_c[:, h:h + 1] - acs_r[h:h + 1, :], NEG))
                ydiags.append(_nn((cb * dec).astype(bf16), xdt_bf[:, h * P:(h + 1) * P]))
            dh = _tn(xd_all[:, g * hpg * P:(g + 1) * hpg * P], bg)
            for c4 in range(hpg):
                h = g * hpg + c4
                hst[h] = hst[h] * jnp.exp(alast_c[:, h:h + 1]) + dh[c4 * P:(c4 + 1) * P]

        y = (jnp.concatenate(ydiags, axis=1) + jnp.concatenate(yoffs, axis=1) * per_head_lanes(jnp.exp(acs_c))
             + dskx_ref[...] * xs_all)
        y_ref[...] = _rms(y * jax.nn.silu(z_ref[...]), nrm_ref[...]).astype(y_ref.dtype)

        @pl.when(c == nc - 1)
        def _():
            st_ref[...] = hst[...]

    tok = lambda w: pl.BlockSpec((L, w), lambda b, c: (b * nc + c, 0))
    cst = lambda shp: pl.BlockSpec(shp, lambda b, c: tuple(0 for _ in shp))
    return pl.pallas_call(
        body, grid=(nb, nc),
        in_specs=[tok(C), tok(d_ssm), tok(LANES), cst((kw, C)), cst((1, C)), cst((1, LANES)), cst((1, LANES)),
                  cst((1, d_ssm)), cst((H, LANES)), cst((H, LANES)), cst((1, d_ssm)), cst((LANES, d_ssm))],
        out_specs=[tok(d_ssm), pl.BlockSpec((None, H, P, N), lambda b, c: (b, 0, 0, 0)),
                   pl.BlockSpec((None, kw - 1, C), lambda b, c: (b, 0, 0))],
        out_shape=[jax.ShapeDtypeStruct((T, d_ssm), bf16), jax.ShapeDtypeStruct((nb, H, P, N), f32),
                   jax.ShapeDtypeStruct((nb, kw - 1, C), f32)],
        scratch_shapes=[pltpu.VMEM((L + 8, C), f32), pltpu.VMEM((H, P, N), f32)],
        compiler_params=_params(("parallel", "arbitrary"), 32), name="ssd_prompt",
    )(xbc, z, small, conv_w, conv_b.reshape(1, C), pad(dt_bias), pad(a_log), jnp.repeat(d_skip, P).reshape(1, d_ssm),
      col(dt_bias), col(a_log), ssm_norm.reshape(1, d_ssm), rep)


def _ssd_sample(xbc, z, small, state, conv_prev, layer, conv_w, conv_b, dt_bias, a_log, d_skip, ssm_norm, dec_seq):
    T, C = xbc.shape
    S = T // dec_seq
    d_ssm = z.shape[1]
    H = d_ssm // SSM_HEAD_DIM
    P, N = SSM_HEAD_DIM, SSM_STATE
    hpg = H // SSM_GROUPS
    kw = conv_w.shape[0]
    Lt = dec_seq
    tmaj = lambda a: a.reshape(S, Lt, a.shape[1]).transpose(1, 0, 2)

    def conv_body(x_ref, prev_ref, sm_ref, cw_ref, cb_ref, xt_ref, smt_ref, cv_ref):
        rows = [prev_ref[k] for k in range(kw - 1)] + [x_ref[t] for t in range(Lt)]
        for t in range(Lt):
            acc = cb_ref[...] + cw_ref[0:1] * rows[t]
            for k in range(1, kw):
                acc += cw_ref[k:k + 1] * rows[t + k]
            xt_ref[t] = jax.nn.silu(acc).T
            smt_ref[t] = sm_ref[t].T
        for k in range(kw - 1):
            cv_ref[k] = rows[Lt + k]

    xt, smt, conv_new = pl.pallas_call(
        conv_body, grid=(1,),
        in_specs=[pl.BlockSpec((Lt, S, C), lambda i: (0, 0, 0)), pl.BlockSpec((None, kw - 1, S, C), lambda i: (layer, 0, 0, 0)),
                  pl.BlockSpec((Lt, S, LANES), lambda i: (0, 0, 0)), pl.BlockSpec((kw, C), lambda i: (0, 0)),
                  pl.BlockSpec((1, C), lambda i: (0, 0))],
        out_specs=[pl.BlockSpec((Lt, C, S), lambda i: (0, 0, 0)), pl.BlockSpec((Lt, LANES, S), lambda i: (0, 0, 0)),
                   pl.BlockSpec((kw - 1, S, C), lambda i: (0, 0, 0))],
        out_shape=[jax.ShapeDtypeStruct((Lt, C, S), f32), jax.ShapeDtypeStruct((Lt, LANES, S), f32),
                   jax.ShapeDtypeStruct((kw - 1, S, C), f32)],
        compiler_params=_params(("arbitrary",), 32), name="ssd_sample_conv",
    )(tmaj(xbc), conv_prev, tmaj(small), conv_w, conv_b.reshape(1, C))

    def state_body(st_ref, xt_ref, smt_ref, dtb_ref, alog_ref, dsk_ref, nst_ref, y_ref, xdt_s):
        h = pl.program_id(0)
        g = h // hpg
        xoff = pl.multiple_of(h * P, P)
        boff = pl.multiple_of(d_ssm + g * N, N)
        coff = pl.multiple_of(d_ssm + (SSM_GROUPS + g) * N, N)
        a = -jnp.exp(alog_ref[pl.ds(h, 1), :])
        dsk = dsk_ref[pl.ds(h, 1), :]
        decs, bs, cs = [], [], []
        for t in range(Lt):
            dt = jax.nn.softplus(smt_ref[t, pl.ds(h, 1), :] + dtb_ref[pl.ds(h, 1), :])
            x = xt_ref[t, pl.ds(xoff, P), :]
            decs.append(jnp.exp(dt * a))
            xdt_s[t] = x * dt
            bs.append(xt_ref[t, pl.ds(boff, N), :])
            cs.append(xt_ref[t, pl.ds(coff, N), :])
            y_ref[t] = dsk * x

        def step(p, carry):
            r0 = pl.multiple_of(p * N, N)
            hp = st_ref[pl.ds(r0, N), :]
            for t in range(Lt):
                hp = hp * decs[t] + xdt_s[t, pl.ds(p, 1), :] * bs[t]
                y_ref[t, pl.ds(p, 1), :] += jnp.sum(cs[t] * hp, axis=0, keepdims=True)
            nst_ref[pl.ds(r0, N), :] = hp
            return carry

        lax.fori_loop(0, P, step, 0)

    colS = lambda v: jnp.broadcast_to(v[:, None], (H, S))
    new_state, yt = pl.pallas_call(
        state_body, grid=(H,),
        in_specs=[pl.BlockSpec((None, None, P * N, S), lambda h: (layer, h, 0, 0)),
                  pl.BlockSpec((Lt, C, S), lambda h: (0, 0, 0)), pl.BlockSpec((Lt, LANES, S), lambda h: (0, 0, 0)),
                  pl.BlockSpec((H, S), lambda h: (0, 0)), pl.BlockSpec((H, S), lambda h: (0, 0)),
                  pl.BlockSpec((H, S), lambda h: (0, 0))],
        out_specs=[pl.BlockSpec((None, P * N, S), lambda h: (h, 0, 0)), pl.BlockSpec((Lt, P, S), lambda h: (0, h, 0))],
        out_shape=[jax.ShapeDtypeStruct((H, P * N, S), f32), jax.ShapeDtypeStruct((Lt, d_ssm, S), f32)],
        scratch_shapes=[pltpu.VMEM((Lt, P, S), f32)],
        compiler_params=_params(("parallel",), 32), name="ssd_sample_state",
    )(state, xt, smt, colS(dt_bias), colS(a_log), colS(d_skip))

    def out_body(yt_ref, z_ref, nrm_ref, o_ref):
        for t in range(Lt):
            y = yt_ref[t].T * jax.nn.silu(z_ref[t])
            o_ref[t] = _rms(y, nrm_ref[...])

    y = pl.pallas_call(
        out_body, grid=(1,),
        in_specs=[pl.BlockSpec((Lt, d_ssm, S), lambda i: (0, 0, 0)), pl.BlockSpec((Lt, S, d_ssm), lambda i: (0, 0, 0)),
                  pl.BlockSpec((1, d_ssm), lambda i: (0, 0))],
        out_specs=pl.BlockSpec((Lt, S, d_ssm), lambda i: (0, 0, 0)),
        out_shape=jax.ShapeDtypeStruct((Lt, S, d_ssm), f32),
        compiler_params=_params(("arbitrary",), 32), name="ssd_sample_out",
    )(yt, tmaj(z), ssm_norm.reshape(1, d_ssm))
    return y.transpose(1, 0, 2).reshape(T, d_ssm), new_state, conv_new


def _slot_block(slot):
    return jnp.where(slot < LANES // 2, 2 * slot, 2 * (slot - LANES // 2) + 1)


def _pool_consts(cmp_wk, cmp_wv, rows):
    r = jnp.arange(rows)
    tile = lambda w: jnp.tile(w.T[:, r % CMP_BLOCK], (N_KV, 1))
    wt = jnp.concatenate([tile(cmp_wk), tile(cmp_wv)], axis=0)
    blk = r // CMP_BLOCK
    slot = jnp.where(blk % 2 == 0, blk // 2, LANES // 2 + blk // 2)
    seg = (slot[:, None] == jnp.arange(LANES)[None, :]).astype(bf16)
    return wt, seg


def _expand_const(n_keys):
    blk = jnp.arange(n_keys) // SEL_BLOCK
    return (jnp.arange(LANES)[:, None] == blk[None, :]).astype(bf16)


def _cmp_pool_prompt(kvt, wt, seg):
    nb, _, L = kvt.shape
    F = 2 * N_KV * HEAD_DIM

    def body(kv_ref, wt_ref, seg_ref, segt_ref, pool_ref, poolt_ref):
        hi, lo = _split2(kv_ref[...] * wt_ref[...])
        poolt_ref[...] = _nn(hi, seg_ref[...]) + _nn(lo, seg_ref[...])
        pool_ref[...] = _nt(segt_ref[...], hi) + _nt(segt_ref[...], lo)

    return pl.pallas_call(
        body, grid=(nb,),
        in_specs=[pl.BlockSpec((None, F, L), lambda b: (b, 0, 0)), pl.BlockSpec((F, L), lambda b: (0, 0)),
                  pl.BlockSpec((L, LANES), lambda b: (0, 0)), pl.BlockSpec((LANES, L), lambda b: (0, 0))],
        out_specs=[pl.BlockSpec((None, LANES, F), lambda b: (b, 0, 0)), pl.BlockSpec((None, F, LANES), lambda b: (b, 0, 0))],
        out_shape=[jax.ShapeDtypeStruct((nb, LANES, F), f32), jax.ShapeDtypeStruct((nb, F, LANES), f32)],
        compiler_params=_params(("parallel",), 32), name="cmp_pool_prompt")(kvt, wt, seg, seg.T)


def _select_blocks(score, valid, blk, n_blocks, axis):
    cnt = jnp.zeros(score.shape, f32)
    for i in range(n_blocks):
        si = score[i:i + 1, :] if axis == 0 else score[:, i:i + 1]
        beat = (si > score) | ((si == score) & (blk > i))
        cnt += jnp.where(beat, 1.0, 0.0)
    return jnp.where((cnt < TOP_N) & valid, 1.0, 0.0)


def _nsa_prompt(q, small, kvt, wint, pool, poolt, expand, gate_b, goff, tq=128):
    T, dq = q.shape
    nb, _, L = kvt.shape
    hd = HEAD_DIM
    H = dq // hd
    hpg = H // N_KV
    F = 2 * N_KV * hd
    nq = L // tq
    scale = hd ** -0.5
    n_sel = L // SEL_BLOCK
    span = tq + WINDOW
    assert L % SEL_TK == 0 and L % tq == 0 and tq % SEL_BLOCK == 0 and n_sel <= LANES // 2 and WINDOW % LANES == 0

    def body(q_ref, sm_ref, kv_ref, win_ref, pool_ref, poolt_ref, e_ref, gb_ref, o_ref, kbf, wbf, bias_ref):
        qi = pl.program_id(1)

        @pl.when(qi == 0)
        def _():
            kbf[...] = kv_ref[...].astype(bf16)
            wbf[:, 0:WINDOW] = jnp.zeros((F, WINDOW), bf16)
            wbf[:, WINDOW:] = win_ref[...].astype(bf16)

        q0 = pl.multiple_of(qi * tq, tq)
        qs = q_ref[...] * scale
        gates = jax.nn.sigmoid(sm_ref[...] + gb_ref[...])
        qpos_c = q0 + lax.broadcasted_iota(jnp.int32, (tq, 1), 0)
        qpos_r = q0 + lax.broadcasted_iota(jnp.int32, (1, tq), 1)
        qpos_c4 = jnp.concatenate([qpos_c] * hpg, axis=0)
        qpos_r4 = jnp.concatenate([qpos_r] * hpg, axis=1)
        slot_r = _slot_block(lax.broadcasted_iota(jnp.int32, (1, LANES), 1))
        slot_c = _slot_block(lax.broadcasted_iota(jnp.int32, (LANES, 1), 0))
        outs = []
        for g in range(N_KV):
            qg = jnp.concatenate([qs[:, (g * hpg + c) * hd:(g * hpg + c + 1) * hd] for c in range(hpg)], axis=0).astype(bf16)
            kct = poolt_ref[g * hd:(g + 1) * hd, :].astype(bf16)
            kc = pool_ref[:, g * hd:(g + 1) * hd].astype(bf16)
            vc = pool_ref[:, (N_KV + g) * hd:(N_KV + g + 1) * hd].astype(bf16)
            e_cmp, r_cmp = _masked_exp(_nn(qg, kct), (slot_r * CMP_BLOCK + CMP_BLOCK - 1) <= qpos_c4)
            o_cmp = _nn(e_cmp.astype(bf16), vc)
            pt = _masked_softmax(_nt(kc, qg), (slot_c * CMP_BLOCK + CMP_BLOCK - 1) <= qpos_r4, axis=0)
            impt = pt[:, 0:tq]
            for c in range(1, hpg):
                impt = impt + pt[:, c * tq:(c + 1) * tq]
            imp = impt[0:n_sel] + impt[LANES // 2:LANES // 2 + n_sel]
            blk = lax.broadcasted_iota(jnp.int32, (n_sel, 1), 0)
            cur = qpos_r // SEL_BLOCK
            valid = blk <= cur
            forced = (blk == 0) | (blk == cur) | (blk == cur - 1)
            score = jnp.where(valid, jnp.where(forced, FORCE_SCORE, imp), NEG)
            selt = _select_blocks(score, valid, blk, n_sel, axis=0)
            sel = jnp.concatenate([selt, jnp.zeros((LANES - n_sel, tq), f32)], axis=0).T
            expd = _nn(sel.astype(bf16), e_ref[...])
            kpos = lax.broadcasted_iota(jnp.int32, (1, L), 1)
            bias_ref[...] = jnp.where((expd > 0.5) & (kpos <= qpos_c), 0.0, NEG)

            def sel_step(j, carry):
                m, l, acc = carry
                k0 = pl.multiple_of(j * SEL_TK, SEL_TK)
                kt = kbf[g * hd:(g + 1) * hd, pl.ds(k0, SEL_TK)]
                vt = kbf[(N_KV + g) * hd:(N_KV + g + 1) * hd, pl.ds(k0, SEL_TK)]
                s = _nn(qg, kt).reshape(hpg, tq, SEL_TK) + bias_ref[:, pl.ds(k0, SEL_TK)][None]
                s = s.reshape(hpg * tq, SEL_TK)
                m_new = jnp.maximum(m, jnp.max(s, axis=-1, keepdims=True))
                alpha = jnp.exp(m - m_new)
                pj = jnp.exp(s - m_new)
                l = alpha * l + jnp.sum(pj, axis=-1, keepdims=True)
                acc = alpha * acc + _nt(pj.astype(bf16), vt)
                return m_new, l, acc

            init = (jnp.full((hpg * tq, 1), NEG, f32), jnp.zeros((hpg * tq, 1), f32), jnp.zeros((hpg * tq, hd), f32))
            _, l_sel, o_sel = lax.fori_loop(0, (q0 + tq + SEL_TK - 1) // SEL_TK, sel_step, init)
            r_sel = 1.0 / l_sel
            kw_ = wbf[g * hd:(g + 1) * hd, pl.ds(q0, span)]
            vw_ = wbf[(N_KV + g) * hd:(N_KV + g + 1) * hd, pl.ds(q0, span)]
            col = lax.broadcasted_iota(jnp.int32, (1, span), 1)
            dist = (lax.broadcasted_iota(jnp.int32, (tq, 1), 0) + WINDOW) - col
            wmask = (dist >= 0) & (dist < WINDOW) & (col + q0 >= WINDOW)
            wmask4 = jnp.concatenate([wmask] * hpg, axis=0)
            e_win, r_win = _masked_exp(_nn(qg, kw_), wmask4)
            o_win = _nt(e_win.astype(bf16), vw_)
            for c in range(hpg):
                h = g * hpg + c
                rows = slice(c * tq, (c + 1) * tq)
                gc, gs, gw = (gates[:, goff + k * H + h:goff + k * H + h + 1] for k in range(3))
                outs.append((gc * r_cmp[rows]) * o_cmp[rows] + (gs * r_sel[rows]) * o_sel[rows] + (gw * r_win[rows]) * o_win[rows])
        o_ref[...] = jnp.concatenate(outs, axis=1).astype(o_ref.dtype)

    tok = lambda w: pl.BlockSpec((tq, w), lambda b, i: (b * nq + i, 0))
    return pl.pallas_call(
        body, grid=(nb, nq),
        in_specs=[tok(dq), tok(LANES), pl.BlockSpec((None, F, L), lambda b, i: (b, 1, 0)),
                  pl.BlockSpec((None, F, L), lambda b, i: (b, 0, 0)),
                  pl.BlockSpec((None, LANES, F), lambda b, i: (b, 0, 0)), pl.BlockSpec((None, F, LANES), lambda b, i: (b, 0, 0)),
                  pl.BlockSpec((LANES, L), lambda b, i: (0, 0)), pl.BlockSpec((1, LANES), lambda b, i: (0, 0))],
        out_specs=tok(dq), out_shape=jax.ShapeDtypeStruct((T, dq), bf16),
        scratch_shapes=[pltpu.VMEM((F, L), bf16), pltpu.VMEM((F, L + WINDOW), bf16), pltpu.VMEM((tq, L), f32)],
        compiler_params=_params(("parallel", "arbitrary"), 48), name="nsa_prompt",
    )(q, small, kvt, wint, pool, poolt, expand, gate_b)


def _nsa_sample(q, kv_new, win_new, small, cache_t, win_t, page_table, layer, wt_page, seg, expand, gate_b, goff, dec_seq):
    T, dq = q.shape
    S, n_pages = page_table.shape
    page = cache_t.shape[-1]
    keep = win_t.shape[-1]
    wd = win_t.shape[2]
    past = n_pages * page
    hd = HEAD_DIM
    H = dq // hd
    hpg = H // N_KV
    F = 2 * N_KV * hd
    G2 = N_KV * hd
    Lt = dec_seq
    SB = SAMPLE_SEQS
    RT = SB * Lt
    RS = hpg * N_KV * Lt
    R = SB * RS
    scale = hd ** -0.5
    n_sel = past // SEL_BLOCK + 1
    new_blk = past // SEL_BLOCK
    k_start = past - keep
    assert N_KV == 2 and G2 == LANES and R == LANES and RT % 8 == 0 and S % SB == 0
    assert past % SEL_BLOCK == 0 and Lt <= SEL_BLOCK and n_sel <= LANES // 2
    assert page == LANES and keep % LANES == 0 and keep <= WINDOW and Lt <= LANES and wd == 2 * G2

    r_ = jnp.arange(R)
    seq_r, c_r, g_r, t_r = r_ // RS, (r_ // (N_KV * Lt)) % hpg, (r_ // Lt) % N_KV, r_ % Lt
    z_of_r = ((g_r * hpg + c_r) * SB + seq_r) * Lt + t_r
    perm = (z_of_r[:, None] == jnp.arange(R)[None, :]).astype(bf16)
    rep = jnp.stack([(jnp.arange(LANES)[:, None] == goff + k * H + jnp.arange(dq)[None, :] // hd) for k in range(3)]).astype(bf16)
    shift = (jnp.arange(Lt)[:, None] + (LANES - Lt) == jnp.arange(LANES)[None, :]).astype(bf16)

    def body(pt_ref, q_ref, kvn_ref, wn_ref, sm_ref, *refs):
        pages = [refs[sq * n_pages:(sq + 1) * n_pages] for sq in range(SB)]
        win_ref, wt_ref, seg_ref, e_ref, gb_ref, shift_ref, perm_ref, rep_ref, o_ref, wout_ref = refs[SB * n_pages:]
        qs = q_ref[...] * scale
        kvn = kvn_ref[...]
        wn = wn_ref[...]
        gates = jax.nn.sigmoid(sm_ref[...] + gb_ref[...])
        lane = lax.broadcasted_iota(jnp.int32, (1, LANES), 1)
        rowi = lax.broadcasted_iota(jnp.int32, (R, 1), 0)
        g_row = (rowi // Lt) % N_KV
        t_row = rowi % Lt
        qpos = past + t_row

        staged = []
        for g in range(N_KV):
            for c in range(hpg):
                pair = jnp.concatenate([qs[:, c * hd:(c + 1) * hd], qs[:, (hpg + c) * hd:(hpg + c + 1) * hd]], axis=1)
                staged.append(jnp.where((lane >= g * hd) & (lane < (g + 1) * hd), pair, 0.0))
        qbd32 = _nn(perm_ref[...], jnp.concatenate(staged, axis=0).astype(bf16))
        qbd = qbd32.astype(bf16)
        seq_rows = lambda a, sq: a[sq * RS:(sq + 1) * RS]
        per_seq = lambda fn: jnp.concatenate([fn(sq) for sq in range(SB)], axis=0)
        own_lanes = lambda o: jnp.where(g_row == 0, o, pltpu.roll(o, hd, 1))[:, 0:hd]

        poolt = []
        for sq in range(SB):
            hi, lo = _split2(jnp.concatenate([pages[sq][p][0:F, :] * wt_ref[...] for p in range(n_pages)], axis=1))
            poolt.append(_nn(hi, seg_ref[...]) + _nn(lo, seg_ref[...]))

        s_c = per_seq(lambda sq: _nn(seq_rows(qbd, sq), poolt[sq][0:G2].astype(bf16)))
        p = _masked_softmax(s_c, (_slot_block(lane) * CMP_BLOCK + CMP_BLOCK - 1) <= qpos)
        p_bf = p.astype(bf16)
        o_cmp = own_lanes(per_seq(lambda sq: _nt(seq_rows(p_bf, sq), poolt[sq][G2:2 * G2].astype(bf16))))

        def imp_seq(sq):
            ps = seq_rows(p, sq)
            acc = ps[0:N_KV * Lt]
            for c in range(1, hpg):
                acc = acc + ps[c * N_KV * Lt:(c + 1) * N_KV * Lt]
            return acc
        imp = per_seq(imp_seq)
        imp = imp + pltpu.roll(imp, LANES // 2, 1)
        nrow = SB * N_KV * Lt
        qpos_i = past + lax.broadcasted_iota(jnp.int32, (nrow, 1), 0) % Lt
        cur = qpos_i // SEL_BLOCK
        valid = (lane <= cur) & (lane < n_sel)
        forced = (lane == 0) | (lane == cur) | (lane == cur - 1)
        score = jnp.where(valid, jnp.where(forced, FORCE_SCORE, imp), NEG)
        sel = _select_blocks(score, valid, lane, n_sel, axis=1)
        expd = _nn(sel.astype(bf16), e_ref[...])
        kpos = lax.broadcasted_iota(jnp.int32, (1, past), 1)
        bias = jnp.where((expd > 0.5) & (kpos <= qpos_i), 0.0, NEG)
        gt = N_KV * Lt
        tile_c = lambda a: per_seq(lambda sq: jnp.concatenate([a[sq * gt:(sq + 1) * gt]] * hpg, axis=0))
        sel_new = tile_c(sel[:, new_blk:new_blk + 1]) > 0.5

        def new_rows(src, lo_lane):
            return [per_seq(lambda sq: jnp.broadcast_to(src[sq * Lt + t:sq * Lt + t + 1, lo_lane:lo_lane + G2], (RS, G2)))
                    for t in range(Lt)]

        def attend(s_past, k_new, v_new, ok_new, v_past_fn):
            s_new = [jnp.where(ok_new & (t_row >= t), jnp.sum(qbd32 * k_new[t], axis=-1, keepdims=True), NEG) for t in range(Lt)]
            m = jnp.max(s_past, axis=-1, keepdims=True)
            for s_t in s_new:
                m = jnp.maximum(m, s_t)
            e_past = jnp.exp(s_past - m)
            den = jnp.sum(e_past, axis=-1, keepdims=True)
            e_bf = e_past.astype(bf16)
            acc = per_seq(lambda sq: v_past_fn(sq, seq_rows(e_bf, sq)))
            for t, s_t in enumerate(s_new):
                e_t = jnp.exp(s_t - m)
                den = den + e_t
                acc = acc + e_t * v_new[t]
            return own_lanes(acc) * (1.0 / den)

        s_sel = per_seq(lambda sq: jnp.concatenate([_nn(seq_rows(qbd, sq), pages[sq][p][F:F + G2, :].astype(bf16))
                                                    for p in range(n_pages)], axis=1)) + tile_c(bias)

        def v_sel(sq, e):
            acc = _nt(e[:, 0:page], pages[sq][0][F + G2:F + 2 * G2, :].astype(bf16))
            for p in range(1, n_pages):
                acc += _nt(e[:, p * page:(p + 1) * page], pages[sq][p][F + G2:F + 2 * G2, :].astype(bf16))
            return acc

        o_sel = attend(s_sel, new_rows(kvn, F), new_rows(kvn, F + G2), sel_new, v_sel)
        dist = qpos - (k_start + lax.broadcasted_iota(jnp.int32, (1, keep), 1))
        s_win = jnp.where((dist >= 0) & (dist < WINDOW),
                          per_seq(lambda sq: _nn(seq_rows(qbd, sq), win_ref[sq, 0:G2, :].astype(bf16))), NEG)
        o_win = attend(s_win, new_rows(wn, 0), new_rows(wn, G2), jnp.full((R, 1), True),
                       lambda sq, e: _nt(e, win_ref[sq, G2:2 * G2, :].astype(bf16)))

        def token_major(o):
            hi, lo = _split2(o)
            st = _tn(perm_ref[...], hi) + _tn(perm_ref[...], lo)
            return jnp.concatenate([st[k * RT:(k + 1) * RT] for k in range(N_KV * hpg)], axis=1)

        def gate(k):
            hi, lo = _split2(gates)
            return _nn(hi, rep_ref[k]) + _nn(lo, rep_ref[k])

        o_ref[...] = gate(0) * token_major(o_cmp) + gate(1) * token_major(o_sel) + gate(2) * token_major(o_win)

        for sq in range(SB):
            rolled = pltpu.roll(win_ref[sq], keep - Lt, 1)
            hi, mid, lo = _split3(wn[sq * Lt:(sq + 1) * Lt])
            new_t = _tn(hi, shift_ref[...]) + _tn(mid, shift_ref[...]) + _tn(lo, shift_ref[...])
            last = jnp.where(lane >= LANES - Lt, new_t, rolled[:, keep - LANES:keep])
            if keep > LANES:
                wout_ref[sq, :, 0:keep - LANES] = rolled[:, 0:keep - LANES]
            wout_ref[sq, :, keep - LANES:keep] = last

    tok = lambda w: pl.BlockSpec((RT, w), lambda i, pt: (i, 0))
    cst = lambda shp: pl.BlockSpec(shp, lambda i, pt: tuple(0 for _ in shp))
    page_spec = lambda sq, p: pl.BlockSpec((None, None, 2 * F, page), lambda i, pt: (layer, pt[i * SB + sq, p], 0, 0))
    grid_spec = pltpu.PrefetchScalarGridSpec(
        num_scalar_prefetch=1, grid=(S // SB,),
        in_specs=[tok(dq), tok(2 * F), tok(wd), tok(LANES)] + [page_spec(sq, p) for sq in range(SB) for p in range(n_pages)] + [
            pl.BlockSpec((None, SB, wd, keep), lambda i, pt: (layer, i, 0, 0)),
            cst((F, page)), cst((past, LANES)), cst((LANES, past)), cst((1, LANES)), cst((Lt, LANES)), cst((R, R)),
            cst((3, LANES, dq))],
        out_specs=[tok(dq), pl.BlockSpec((SB, wd, keep), lambda i, pt: (i, 0, 0))])
    return pl.pallas_call(
        body, grid_spec=grid_spec,
        out_shape=[jax.ShapeDtypeStruct((T, dq), f32), jax.ShapeDtypeStruct((S, wd, keep), f32)],
        compiler_params=_params(("parallel",), 56), name="nsa_sample",
    )(page_table, q, kv_new, win_new, small, *([cache_t] * (SB * n_pages)), win_t, wt_page, seg, expand, gate_b, shift, perm, rep)


def kernel(x_prompt, x_sample, mem_prompt, cache_nsa_kv, cache_mem_kv, state_nsa_win, state_ssm, state_conv, page_table, norm_mix_pre, w_in, conv_w, conv_b, dt_bias, a_log, d_skip, ssm_norm, cmp_wk, cmp_wv, gate_b, w_out, norm_mix_post, norm_mem_src, w_mem_q, w_mem_k, w_mem_v, w_mem_o, norm_mem_pre, norm_mem_post, norm_ffn_pre, w_up, w_down, norm_ffn_post):
    B, L, D = x_prompt.shape
    S, Lt, _ = x_sample.shape
    depth = w_in.shape[0]
    M = mem_prompt.shape[1]
    d_ssm = ssm_norm.shape[1]
    C = conv_w.shape[2]
    H = dt_bias.shape[1]
    P, N = state_ssm.shape[3], state_ssm.shape[4]
    d_att = w_out.shape[1] - d_ssm
    Ha = d_att // HEAD_DIM
    n_pool, page = cache_nsa_kv.shape[1], cache_nsa_kv.shape[2]
    keep = state_nsa_win.shape[2]
    kvw = 4 * N_KV * HEAD_DIM
    winw = 2 * N_KV * HEAD_DIM
    past = page_table.shape[1] * page
    assert H + 3 * Ha <= LANES

    cache_t = cache_nsa_kv.transpose(0, 1, 3, 4, 5, 2).reshape(depth, n_pool, kvw, page)
    win_t = state_nsa_win.transpose(0, 1, 3, 4, 5, 2).reshape(depth, S, winw, keep)
    ssm_t = state_ssm.transpose(0, 2, 3, 4, 1).reshape(depth, H, P * N, S)
    conv_t = state_conv.transpose(0, 2, 1, 3)

    o1 = d_ssm
    o2 = o1 + C
    o3 = o2 + H
    o4 = o3 + d_att
    o5 = o4 + kvw
    o6 = o5 + winw
    n_gate = 3 * Ha

    expand_p = _expand_const(L)
    expand_s = _expand_const(past)
    xp = x_prompt.reshape(B * L, D)
    xs = x_sample.reshape(S * Lt, D)
    mem = mem_prompt.reshape(B * M, D)
    kv_p, kv_s, win_p, win_s, ssm_p, ssm_s, conv_p, conv_s, mkv_out = [], [], [], [], [], [], [], [], []
    for l in range(depth):
        wl = w_in[l].astype(bf16)
        w_z, w_xbc, w_q = wl[:, :o1], wl[:, o1:o2], wl[:, o3:o4]
        w_kv, w_win = wl[:, o4:o5], wl[:, o5:o6]
        w_small = jnp.concatenate([wl[:, o2:o3], wl[:, o6:o6 + n_gate], jnp.zeros((D, LANES - H - n_gate), bf16)], axis=1)
        gb = jnp.zeros((1, LANES), f32).at[0, H:H + n_gate].set(gate_b[l])
        wo = w_out[l].astype(bf16)
        wt_full, seg_full = _pool_consts(cmp_wk[l], cmp_wv[l], L)
        wt_page, seg_page = _pool_consts(cmp_wk[l], cmp_wv[l], past)

        mkv, = _norm_proj(mem, norm_mem_src[l], [jnp.concatenate([w_mem_k[l], w_mem_v[l]], axis=1).astype(bf16)])
        mkv_out.append(mkv.reshape(B, M, 2, MEM_HEADS, D // MEM_HEADS))

        z, xbc, q, small, kvt, wint = _norm_proj(xp, norm_mix_pre[l], [w_z, w_xbc, w_q, w_small], [w_kv.T, w_win.T], seq=L,
                                                 row_dtypes=[f32, f32, bf16, f32])
        y_ssd, st_new, cv_new = _ssd_prompt(xbc, z, small, conv_w[l], conv_b[l], dt_bias[l], a_log[l], d_skip[l], ssm_norm[l], B)
        pool, poolt = _cmp_pool_prompt(kvt, wt_full, seg_full)
        o_att = _nsa_prompt(q, small, kvt, wint, pool, poolt, expand_p, gb, H)
        xp = _linear_post(xp, [y_ssd, o_att], [wo[:d_ssm], wo[d_ssm:]], norm_mix_post[l])
        kv_p.append(kvt)
        win_p.append(wint[:, :, L - min(WINDOW, L):])
        ssm_p.append(st_new)
        conv_p.append(cv_new)

        z, xbc, q, small, kvr, winr = _norm_proj(xs, norm_mix_pre[l], [w_z, w_xbc, w_q, w_small, w_kv, w_win])
        y_ssd, st_new, cv_new = _ssd_sample(xbc, z, small, ssm_t, conv_t, l, conv_w[l], conv_b[l], dt_bias[l], a_log[l],
                                            d_skip[l], ssm_norm[l], Lt)
        o_att, wnew = _nsa_sample(q, kvr, winr, small, cache_t, win_t, page_table, l, wt_page[:, :page], seg_page,
                                  expand_s, gb, H, Lt)
        xs = _linear_post(xs, [y_ssd, o_att], [wo[:d_ssm], wo[d_ssm:]], norm_mix_post[l])
        kv_s.append(kvr)
        win_s.append(wnew)
        ssm_s.append(st_new)
        conv_s.append(cv_new)

        wq, wmo = w_mem_q[l].astype(bf16), w_mem_o[l].astype(bf16)
        qm, = _norm_proj(xp, norm_mem_pre[l], [wq], row_dtypes=[bf16])
        xp = _linear_post(xp, [_mem_attn_prompt(qm, mkv, L)], [wmo], norm_mem_post[l])
        qm, = _norm_proj(xs, norm_mem_pre[l], [wq])
        xs = _linear_post(xs, [_mem_attn_sample(qm, cache_mem_kv, l, Lt)], [wmo], norm_mem_post[l])

        wu, wd = w_up[l].astype(bf16), w_down[l].astype(bf16)
        xp = _ffn(xp, norm_ffn_pre[l], wu, wd, norm_ffn_post[l], tm=1024)
        xs = _ffn(xs, norm_ffn_pre[l], wu, wd, norm_ffn_post[l], tm=512)

    kv5 = (4, N_KV, HEAD_DIM)
    w5 = (2, N_KV, HEAD_DIM)
    return (xp.reshape(B, L, D), xs.reshape(S, Lt, D),
            jnp.stack(kv_p).reshape(depth, B, *kv5, L).transpose(0, 1, 5, 2, 3, 4),
            jnp.stack(kv_s).reshape(depth, S, Lt, *kv5),
            jnp.stack(win_p).reshape(depth, B, *w5, min(WINDOW, L)).transpose(0, 1, 5, 2, 3, 4),
            jnp.stack(win_s).reshape(depth, S, *w5, keep).transpose(0, 1, 5, 2, 3, 4),
            jnp.stack(ssm_p), jnp.stack(ssm_s).reshape(depth, H, P, N, S).transpose(0, 4, 1, 2, 3),
            jnp.stack(conv_p), jnp.stack(conv_s).transpose(0, 2, 1, 3),
            jnp.stack(mkv_out))
```

```python
import functools

import jax
import jax.numpy as jnp
from jax import lax
from jax.experimental import pallas as pl
from jax.experimental.pallas import tpu as pltpu

f32 = jnp.float32
bf16 = jnp.bfloat16
NEG = float("-inf")

SSM_HEAD_DIM = 64
SSM_GROUPS = 2
SSM_STATE = 64
SSD_CHUNK = 128
HEAD_DIM = 64
N_KV = 2
CMP_BLOCK = 32
SEL_BLOCK = 64
TOP_N = 16
WINDOW = 512
FORCE_SCORE = 1.0e4
MEM_HEADS = 4
EPS = 1e-6

LANES = 128
V7X_VMEM_BYTES = 64 * 1024 * 1024
SEL_TK = 512
SAMPLE_SEQS = 4
ROW_CHUNK = 256


def _params(sem, vmem_mb):
    assert vmem_mb * 1024 * 1024 < V7X_VMEM_BYTES
    return pltpu.CompilerParams(dimension_semantics=sem, vmem_limit_bytes=vmem_mb * 1024 * 1024)


def _nn(a, b):
    return jnp.dot(a, b, preferred_element_type=f32)


def _nt(a, b):
    return lax.dot_general(a, b, (((1,), (1,)), ((), ())), preferred_element_type=f32)


def _tn(a, b):
    return lax.dot_general(a, b, (((0,), (0,)), ((), ())), preferred_element_type=f32)


def _rms(x, g):
    return x * lax.rsqrt(jnp.mean(x * x, axis=-1, keepdims=True) + EPS) * g


def _masked_exp(s, mask, axis=-1):
    s = jnp.where(mask, s, NEG)
    m = jnp.max(s, axis=axis, keepdims=True)
    m = jnp.where(jnp.isfinite(m), m, 0.0)
    e = jnp.exp(s - m)
    d = jnp.sum(e, axis=axis, keepdims=True)
    return e, 1.0 / jnp.where(d > 0, d, 1.0)


def _masked_softmax(s, mask, axis=-1):
    e, r = _masked_exp(s, mask, axis)
    return e * r


def _split2(x):
    hi = x.astype(bf16)
    lo = (x - hi.astype(f32)).astype(bf16)
    return hi, lo


def _split3(x):
    hi = x.astype(bf16)
    r = x - hi.astype(f32)
    mid = r.astype(bf16)
    lo = (r - mid.astype(f32)).astype(bf16)
    return hi, mid, lo


def _norm_proj(x, g, ws, wts=(), seq=None, tm=512, row_dtypes=None):
    T, D = x.shape
    tm = min(tm, T)
    assert T % tm == 0
    ws, wts = list(ws), list(wts)
    n_row = len(ws)
    tps = (seq // tm) if wts else 1

    def body(x_ref, g_ref, *refs):
        w_refs, o_refs = refs[:len(ws) + len(wts)], refs[len(ws) + len(wts):]
        for r0 in range(0, tm, ROW_CHUNK):
            rows = slice(r0, min(r0 + ROW_CHUNK, tm))
            xn = _rms(x_ref[rows, :], g_ref[...]).astype(bf16)
            for k in range(n_row):
                o_refs[k][rows, :] = _nn(xn, w_refs[k][...]).astype(o_refs[k].dtype)
            for k in range(n_row, len(w_refs)):
                o_refs[k][:, rows] = _nt(w_refs[k][...], xn)

    in_specs = [pl.BlockSpec((tm, D), lambda i: (i, 0)), pl.BlockSpec((1, D), lambda i: (0, 0))]
    in_specs += [pl.BlockSpec(w.shape, lambda i: (0, 0)) for w in ws + wts]
    out_specs = [pl.BlockSpec((tm, w.shape[1]), lambda i: (i, 0)) for w in ws]
    out_specs += [pl.BlockSpec((None, w.shape[0], tm), lambda i: (i // tps, 0, i % tps)) for w in wts]
    row_dtypes = row_dtypes or [f32] * n_row
    out_shape = [jax.ShapeDtypeStruct((T, w.shape[1]), dt) for w, dt in zip(ws, row_dtypes)]
    out_shape += [jax.ShapeDtypeStruct((T // seq, w.shape[0], seq), f32) for w in wts]
    return pl.pallas_call(body, grid=(T // tm,), in_specs=in_specs, out_specs=out_specs, out_shape=out_shape,
                          compiler_params=_params(("parallel",), 48), name="norm_proj")(x, g.reshape(1, D), *ws, *wts)


def _linear_post(res, acts, ws, g, tm=512):
    T, D = res.shape
    tm = min(tm, T)
    n = len(acts)

    def body(*refs):
        a_refs, w_refs = refs[:n], refs[n:2 * n]
        g_ref, res_ref, o_ref = refs[2 * n:]
        for r0 in range(0, tm, ROW_CHUNK):
            rows = slice(r0, min(r0 + ROW_CHUNK, tm))
            acc = _nn(a_refs[0][rows, :].astype(bf16), w_refs[0][...])
            for k in range(1, n):
                acc += _nn(a_refs[k][rows, :].astype(bf16), w_refs[k][...])
            o_ref[rows, :] = res_ref[rows, :] + _rms(acc, g_ref[...])

    in_specs = [pl.BlockSpec((tm, a.shape[1]), lambda i: (i, 0)) for a in acts]
    in_specs += [pl.BlockSpec(w.shape, lambda i: (0, 0)) for w in ws]
    in_specs += [pl.BlockSpec((1, D), lambda i: (0, 0)), pl.BlockSpec((tm, D), lambda i: (i, 0))]
    return pl.pallas_call(body, grid=(T // tm,), in_specs=in_specs, out_specs=pl.BlockSpec((tm, D), lambda i: (i, 0)),
                          out_shape=jax.ShapeDtypeStruct((T, D), f32),
                          compiler_params=_params(("parallel",), 40), name="linear_post")(*acts, *ws, g.reshape(1, D), res)


def _ffn(x, g1, wu, wd, g2, tm, tc=2048):
    T, D = x.shape
    F = wu.shape[1]
    tm = min(tm, T)
    nj = F // tc

    def body(x_ref, g1_ref, wu_ref, wd_ref, g2_ref, o_ref, xn_s, acc_s):
        j = pl.program_id(1)

        @pl.when(j == 0)
        def _():
            xn_s[...] = _rms(x_ref[...], g1_ref[...]).astype(bf16)
            acc_s[...] = jnp.zeros_like(acc_s)

        for r0 in range(0, tm, ROW_CHUNK):
            rows = slice(r0, min(r0 + ROW_CHUNK, tm))
            h = _nn(xn_s[rows, :], wu_ref[...])
            h = jnp.square(jnp.maximum(h, 0.0))
            acc_s[rows, :] += _nn(h.astype(bf16), wd_ref[...])

        @pl.when(j == nj - 1)
        def _():
            o_ref[...] = x_ref[...] + _rms(acc_s[...], g2_ref[...])

    return pl.pallas_call(
        body, grid=(T // tm, nj),
        in_specs=[pl.BlockSpec((tm, D), lambda i, j: (i, 0)), pl.BlockSpec((1, D), lambda i, j: (0, 0)),
                  pl.BlockSpec((D, tc), lambda i, j: (0, j)), pl.BlockSpec((tc, D), lambda i, j: (j, 0)),
                  pl.BlockSpec((1, D), lambda i, j: (0, 0))],
        out_specs=pl.BlockSpec((tm, D), lambda i, j: (i, 0)),
        out_shape=jax.ShapeDtypeStruct((T, D), f32),
        scratch_shapes=[pltpu.VMEM((tm, D), bf16), pltpu.VMEM((tm, D), f32)],
        compiler_params=_params(("parallel", "arbitrary"), 48), name="ffn")(x, g1.reshape(1, D), wu, wd, g2.reshape(1, D))


def _mem_attn_prompt(q, mkv, seq, tq=512):
    T, D = q.shape
    M = mkv.shape[0] // (T // seq)
    hd = D // MEM_HEADS
    scale = hd ** -0.5
    tps = seq // tq

    def body(q_ref, kv_ref, o_ref):
        for h in range(MEM_HEADS):
            qh = (q_ref[:, h * hd:(h + 1) * hd] * scale).astype(bf16)
            kh = kv_ref[:, h * hd:(h + 1) * hd].astype(bf16)
            vh = kv_ref[:, D + h * hd:D + (h + 1) * hd].astype(bf16)
            s = _nt(qh, kh)
            e = jnp.exp(s - jnp.max(s, axis=-1, keepdims=True))
            d = jnp.sum(e, axis=-1, keepdims=True)
            o_ref[:, h * hd:(h + 1) * hd] = (_nn(e.astype(bf16), vh) * (1.0 / d)).astype(o_ref.dtype)

    return pl.pallas_call(
        body, grid=(T // tq,),
        in_specs=[pl.BlockSpec((tq, D), lambda i: (i, 0)), pl.BlockSpec((M, 2 * D), lambda i: (i // tps, 0))],
        out_specs=pl.BlockSpec((tq, D), lambda i: (i, 0)), out_shape=jax.ShapeDtypeStruct((T, D), bf16),
        compiler_params=_params(("parallel",), 40), name="mem_attn_prompt")(q, mkv)


def _mem_attn_sample(q, cache, layer, dec_seq):
    T, D = q.shape
    _, S, M, _, H, hd = cache.shape
    scale = hd ** -0.5
    spb = 8 // dec_seq
    assert 8 % dec_seq == 0 and S % spb == 0
    ncp = spb * 2 * H

    def body(q_ref, c_ref, o_ref, buf, sem):
        i = pl.program_id(0)
        n = pl.num_programs(0)

        def copies(step, slot):
            return [pltpu.make_async_copy(c_ref.at[layer, step * spb + s, :, kv, h, :], buf.at[slot, (s * 2 + kv) * H + h],
                                          sem.at[slot, (s * 2 + kv) * H + h])
                    for s in range(spb) for kv in range(2) for h in range(H)]

        @pl.when(i == 0)
        def _():
            for cp in copies(0, 0):
                cp.start()

        @pl.when(i + 1 < n)
        def _():
            for cp in copies(i + 1, (i + 1) % 2):
                cp.start()

        slot = i % 2
        for cp in copies(i, slot):
            cp.wait()
        pairs = [(s, h) for s in range(spb) for h in range(H)]
        qsc = (q_ref[...] * scale).astype(bf16)
        sc = jnp.concatenate([_nt(qsc[s * dec_seq:(s + 1) * dec_seq, h * hd:(h + 1) * hd], buf[slot, (s * 2) * H + h].astype(bf16))
                              for s, h in pairs], axis=0)
        e = jnp.exp(sc - jnp.max(sc, axis=-1, keepdims=True))
        r = 1.0 / jnp.sum(e, axis=-1, keepdims=True)
        for k, (s, h) in enumerate(pairs):
            rows = slice(k * dec_seq, (k + 1) * dec_seq)
            o_ref[s * dec_seq:(s + 1) * dec_seq, h * hd:(h + 1) * hd] = (
                _nn(e[rows].astype(bf16), buf[slot, (s * 2 + 1) * H + h].astype(bf16)) * r[rows])

    return pl.pallas_call(
        body, grid=(S // spb,),
        in_specs=[pl.BlockSpec((8, D), lambda i: (i, 0)), pl.BlockSpec(memory_space=pl.ANY)],
        out_specs=pl.BlockSpec((8, D), lambda i: (i, 0)), out_shape=jax.ShapeDtypeStruct((T, D), f32),
        scratch_shapes=[pltpu.VMEM((2, ncp, M, hd), f32), pltpu.SemaphoreType.DMA((2, ncp))],
        compiler_params=_params(("arbitrary",), 40), name="mem_attn_sample")(q, cache)


def _cumsum_steps(n):
    k, out = 1, []
    while k < n:
        out.append(k)
        k *= 2
    return out


def _ssd_prompt(xbc, z, small, conv_w, conv_b, dt_bias, a_log, d_skip, ssm_norm, nb):
    T, C = xbc.shape
    d_ssm = z.shape[1]
    H = d_ssm // SSM_HEAD_DIM
    P, N, L = SSM_HEAD_DIM, SSM_STATE, SSD_CHUNK
    hpg = H // SSM_GROUPS
    kw = conv_w.shape[0]
    seq = T // nb
    nc = seq // L
    assert seq % L == 0 and kw - 1 <= 8

    pad = lambda v: jnp.zeros((1, LANES), f32).at[0, :H].set(v)
    col = lambda v: jnp.broadcast_to(v[:, None], (H, LANES))

    rep = (jnp.arange(LANES)[:, None] == jnp.arange(d_ssm)[None, :] // P).astype(bf16)

    def body(xbc_ref, z_ref, sm_ref, cw_ref, cb_ref, dtb_r, alog_r, dskx_ref, dtb_c, alog_c, nrm_ref, rep_ref,
             y_ref, st_ref, cv_ref, xpad, hst):
        c = pl.program_id(1)

        @pl.when(c == 0)
        def _():
            xpad[0:8] = jnp.zeros((8, C), f32)
            hst[...] = jnp.zeros_like(hst)

        xpad[8:8 + L] = xbc_ref[...]
        acc = cb_ref[...] + cw_ref[0:1] * xpad[9 - kw:9 - kw + L]
        for k in range(1, kw):
            acc += cw_ref[k:k + 1] * xpad[9 - kw + k:9 - kw + k + L]
        xc = jax.nn.silu(acc)
        xpad[0:8] = xpad[L:L + 8]

        @pl.when(c == nc - 1)
        def _():
            cv_ref[...] = xpad[9 - kw:8]

        sm = sm_ref[...]
        dt_c = jax.nn.softplus(sm + dtb_r[...])
        acs_c = dt_c * (-jnp.exp(alog_r[...]))
        row = lax.broadcasted_iota(jnp.int32, (L, LANES), 0)
        for k in _cumsum_steps(L):
            acs_c = acs_c + jnp.where(row >= k, pltpu.roll(acs_c, k, 0), 0.0)
        dt_r = jax.nn.softplus(sm.T[0:H] + dtb_c[...])
        acs_r = dt_r * (-jnp.exp(alog_c[...]))
        lane = lax.broadcasted_iota(jnp.int32, (H, L), 1)
        for k in _cumsum_steps(L):
            acs_r = acs_r + jnp.where(lane >= k, pltpu.roll(acs_r, k, 1), 0.0)

        def per_head_lanes(v):
            hi, lo = _split2(v)
            return _nn(hi, rep_ref[...]) + _nn(lo, rep_ref[...])

        alast_c = acs_c[L - 1:L, :]
        xs_all = xc[:, 0:d_ssm]
        xdt_all = xs_all * per_head_lanes(dt_c)
        xd_all = (xdt_all * per_head_lanes(jnp.exp(alast_c - acs_c))).astype(bf16)
        xdt_bf = xdt_all.astype(bf16)
        causal = lax.broadcasted_iota(jnp.int32, (L, L), 0) >= lax.broadcasted_iota(jnp.int32, (L, L), 1)
        ydiags, yoffs = [], []
        for g in range(SSM_GROUPS):
            bg = xc[:, d_ssm + g * N:d_ssm + (g + 1) * N].astype(bf16)
            cg = xc[:, d_ssm + (SSM_GROUPS + g) * N:d_ssm + (SSM_GROUPS + g + 1) * N].astype(bf16)
            cb = _nt(cg, bg)
            hg = hst[g * hpg:(g + 1) * hpg].reshape(hpg * P, N)
            yoffs.append(_nt(cg, hg.astype(bf16)))
            for c4 in range(hpg):
                h = g * hpg + c4
                dec = jnp.exp(jnp.where(causal, acs_c[:, h:h + 1] - acs_r[h:h + 1, :], NEG))
                ydiags.append(_nn((cb * dec).astype(bf16), xdt_bf[:, h * P:(h + 1) * P]))
            dh = _tn(xd_all[:, g * hpg * P:(g + 1) * hpg * P], bg)
            for c4 in range(hpg):
                h = g * hpg + c4
                hst[h] = hst[h] * jnp.exp(alast_c[:, h:h + 1]) + dh[c4 * P:(c4 + 1) * P]

        y = (jnp.concatenate(ydiags, axis=1) + jnp.concatenate(yoffs, axis=1) * per_head_lanes(jnp.exp(acs_c))
             + dskx_ref[...] * xs_all)
        y_ref[...] = _rms(y * jax.nn.silu(z_ref[...]), nrm_ref[...]).astype(y_ref.dtype)

        @pl.when(c == nc - 1)
        def _():
            st_ref[...] = hst[...]

    tok = lambda w: pl.BlockSpec((L, w), lambda b, c: (b * nc + c, 0))
    cst = lambda shp: pl.BlockSpec(shp, lambda b, c: tuple(0 for _ in shp))
    return pl.pallas_call(
        body, grid=(nb, nc),
        in_specs=[tok(C), tok(d_ssm), tok(LANES), cst((kw, C)), cst((1, C)), cst((1, LANES)), cst((1, LANES)),
                  cst((1, d_ssm)), cst((H, LANES)), cst((H, LANES)), cst((1, d_ssm)), cst((LANES, d_ssm))],
        out_specs=[tok(d_ssm), pl.BlockSpec((None, H, P, N), lambda b, c: (b, 0, 0, 0)),
                   pl.BlockSpec((None, kw - 1, C), lambda b, c: (b, 0, 0))],
        out_shape=[jax.ShapeDtypeStruct((T, d_ssm), bf16), jax.ShapeDtypeStruct((nb, H, P, N), f32),
                   jax.ShapeDtypeStruct((nb, kw - 1, C), f32)],
        scratch_shapes=[pltpu.VMEM((L + 8, C), f32), pltpu.VMEM((H, P, N), f32)],
        compiler_params=_params(("parallel", "arbitrary"), 32), name="ssd_prompt",
    )(xbc, z, small, conv_w, conv_b.reshape(1, C), pad(dt_bias), pad(a_log), jnp.repeat(d_skip, P).reshape(1, d_ssm),
      col(dt_bias), col(a_log), ssm_norm.reshape(1, d_ssm), rep)


def _ssd_sample(xbc, z, small, state, conv_prev, layer, conv_w, conv_b, dt_bias, a_log, d_skip, ssm_norm, dec_seq):
    T, C = xbc.shape
    S = T // dec_seq
    d_ssm = z.shape[1]
    H = d_ssm // SSM_HEAD_DIM
    P, N = SSM_HEAD_DIM, SSM_STATE
    hpg = H // SSM_GROUPS
    kw = conv_w.shape[0]
    Lt = dec_seq
    tmaj = lambda a: a.reshape(S, Lt, a.shape[1]).transpose(1, 0, 2)

    def conv_body(x_ref, prev_ref, sm_ref, cw_ref, cb_ref, xt_ref, smt_ref, cv_ref):
        rows = [prev_ref[k] for k in range(kw - 1)] + [x_ref[t] for t in range(Lt)]
        for t in range(Lt):
            acc = cb_ref[...] + cw_ref[0:1] * rows[t]
            for k in range(1, kw):
                acc += cw_ref[k:k + 1] * rows[t + k]
            xt_ref[t] = jax.nn.silu(acc).T
            smt_ref[t] = sm_ref[t].T
        for k in range(kw - 1):
            cv_ref[k] = rows[Lt + k]

    xt, smt, conv_new = pl.pallas_call(
        conv_body, grid=(1,),
        in_specs=[pl.BlockSpec((Lt, S, C), lambda i: (0, 0, 0)), pl.BlockSpec((None, kw - 1, S, C), lambda i: (layer, 0, 0, 0)),
                  pl.BlockSpec((Lt, S, LANES), lambda i: (0, 0, 0)), pl.BlockSpec((kw, C), lambda i: (0, 0)),
                  pl.BlockSpec((1, C), lambda i: (0, 0))],
        out_specs=[pl.BlockSpec((Lt, C, S), lambda i: (0, 0, 0)), pl.BlockSpec((Lt, LANES, S), lambda i: (0, 0, 0)),
                   pl.BlockSpec((kw - 1, S, C), lambda i: (0, 0, 0))],
        out_shape=[jax.ShapeDtypeStruct((Lt, C, S), f32), jax.ShapeDtypeStruct((Lt, LANES, S), f32),
                   jax.ShapeDtypeStruct((kw - 1, S, C), f32)],
        compiler_params=_params(("arbitrary",), 32), name="ssd_sample_conv",
    )(tmaj(xbc), conv_prev, tmaj(small), conv_w, conv_b.reshape(1, C))

    def state_body(st_ref, xt_ref, smt_ref, dtb_ref, alog_ref, dsk_ref, nst_ref, y_ref, xdt_s):
        h = pl.program_id(0)
        g = h // hpg
        xoff = pl.multiple_of(h * P, P)
        boff = pl.multiple_of(d_ssm + g * N, N)
        coff = pl.multiple_of(d_ssm + (SSM_GROUPS + g) * N, N)
        a = -jnp.exp(alog_ref[pl.ds(h, 1), :])
        dsk = dsk_ref[pl.ds(h, 1), :]
        decs, bs, cs = [], [], []
        for t in range(Lt):
            dt = jax.nn.softplus(smt_ref[t, pl.ds(h, 1), :] + dtb_ref[pl.ds(h, 1), :])
            x = xt_ref[t, pl.ds(xoff, P), :]
            decs.append(jnp.exp(dt * a))
            xdt_s[t] = x * dt
            bs.append(xt_ref[t, pl.ds(boff, N), :])
            cs.append(xt_ref[t, pl.ds(coff, N), :])
            y_ref[t] = dsk * x

        def step(p, carry):
            r0 = pl.multiple_of(p * N, N)
            hp = st_ref[pl.ds(r0, N), :]
            for t in range(Lt):
                hp = hp * decs[t] + xdt_s[t, pl.ds(p, 1), :] * bs[t]
                y_ref[t, pl.ds(p, 1), :] += jnp.sum(cs[t] * hp, axis=0, keepdims=True)
            nst_ref[pl.ds(r0, N), :] = hp
            return carry

        lax.fori_loop(0, P, step, 0)

    colS = lambda v: jnp.broadcast_to(v[:, None], (H, S))
    new_state, yt = pl.pallas_call(
        state_body, grid=(H,),
        in_specs=[pl.BlockSpec((None, None, P * N, S), lambda h: (layer, h, 0, 0)),
                  pl.BlockSpec((Lt, C, S), lambda h: (0, 0, 0)), pl.BlockSpec((Lt, LANES, S), lambda h: (0, 0, 0)),
                  pl.BlockSpec((H, S), lambda h: (0, 0)), pl.BlockSpec((H, S), lambda h: (0, 0)),
                  pl.BlockSpec((H, S), lambda h: (0, 0))],
        out_specs=[pl.BlockSpec((None, P * N, S), lambda h: (h, 0, 0)), pl.BlockSpec((Lt, P, S), lambda h: (0, h, 0))],
        out_shape=[jax.ShapeDtypeStruct((H, P * N, S), f32), jax.ShapeDtypeStruct((Lt, d_ssm, S), f32)],
        scratch_shapes=[pltpu.VMEM((Lt, P, S), f32)],
        compiler_params=_params(("parallel",), 32), name="ssd_sample_state",
    )(state, xt, smt, colS(dt_bias), colS(a_log), colS(d_skip))

    def out_body(yt_ref, z_ref, nrm_ref, o_ref):
        for t in range(Lt):
            y = yt_ref[t].T * jax.nn.silu(z_ref[t])
            o_ref[t] = _rms(y, nrm_ref[...])

    y = pl.pallas_call(
        out_body, grid=(1,),
        in_specs=[pl.BlockSpec((Lt, d_ssm, S), lambda i: (0, 0, 0)), pl.BlockSpec((Lt, S, d_ssm), lambda i: (0, 0, 0)),
                  pl.BlockSpec((1, d_ssm), lambda i: (0, 0))],
        out_specs=pl.BlockSpec((Lt, S, d_ssm), lambda i: (0, 0, 0)),
        out_shape=jax.ShapeDtypeStruct((Lt, S, d_ssm), f32),
        compiler_params=_params(("arbitrary",), 32), name="ssd_sample_out",
    )(yt, tmaj(z), ssm_norm.reshape(1, d_ssm))
    return y.transpose(1, 0, 2).reshape(T, d_ssm), new_state, conv_new


def _slot_block(slot):
    return jnp.where(slot < LANES // 2, 2 * slot, 2 * (slot - LANES // 2) + 1)


def _pool_consts(cmp_wk, cmp_wv, rows):
    r = jnp.arange(rows)
    tile = lambda w: jnp.tile(w.T[:, r % CMP_BLOCK], (N_KV, 1))
    wt = jnp.concatenate([tile(cmp_wk), tile(cmp_wv)], axis=0)
    blk = r // CMP_BLOCK
    slot = jnp.where(blk % 2 == 0, blk // 2, LANES // 2 + blk // 2)
    seg = (slot[:, None] == jnp.arange(LANES)[None, :]).astype(bf16)
    return wt, seg


def _expand_const(n_keys):
    blk = jnp.arange(n_keys) // SEL_BLOCK
    return (jnp.arange(LANES)[:, None] == blk[None, :]).astype(bf16)


def _cmp_pool_prompt(kvt, wt, seg):
    nb, _, L = kvt.shape
    F = 2 * N_KV * HEAD_DIM

    def body(kv_ref, wt_ref, seg_ref, segt_ref, pool_ref, poolt_ref):
        hi, lo = _split2(kv_ref[...] * wt_ref[...])
        poolt_ref[...] = _nn(hi, seg_ref[...]) + _nn(lo, seg_ref[...])
        pool_ref[...] = _nt(segt_ref[...], hi) + _nt(segt_ref[...], lo)

    return pl.pallas_call(
        body, grid=(nb,),
        in_specs=[pl.BlockSpec((None, F, L), lambda b: (b, 0, 0)), pl.BlockSpec((F, L), lambda b: (0, 0)),
                  pl.BlockSpec((L, LANES), lambda b: (0, 0)), pl.BlockSpec((LANES, L), lambda b: (0, 0))],
        out_specs=[pl.BlockSpec((None, LANES, F), lambda b: (b, 0, 0)), pl.BlockSpec((None, F, LANES), lambda b: (b, 0, 0))],
        out_shape=[jax.ShapeDtypeStruct((nb, LANES, F), f32), jax.ShapeDtypeStruct((nb, F, LANES), f32)],
        compiler_params=_params(("parallel",), 32), name="cmp_pool_prompt")(kvt, wt, seg, seg.T)


def _select_blocks(score, valid, blk, n_blocks, axis):
    cnt = jnp.zeros(score.shape, f32)
    for i in range(n_blocks):
        si = score[i:i + 1, :] if axis == 0 else score[:, i:i + 1]
        beat = (si > score) | ((si == score) & (blk > i))
        cnt += jnp.where(beat, 1.0, 0.0)
    return jnp.where((cnt < TOP_N) & valid, 1.0, 0.0)


def _nsa_prompt(q, small, kvt, wint, pool, poolt, expand, gate_b, goff, tq=128):
    T, dq = q.shape
    nb, _, L = kvt.shape
    hd = HEAD_DIM
    H = dq // hd
    hpg = H // N_KV
    F = 2 * N_KV * hd
    nq = L // tq
    scale = hd ** -0.5
    n_sel = L // SEL_BLOCK
    span = tq + WINDOW
    assert L % SEL_TK == 0 and L % tq == 0 and tq % SEL_BLOCK == 0 and n_sel <= LANES // 2 and WINDOW % LANES == 0

    def body(q_ref, sm_ref, kv_ref, win_ref, pool_ref, poolt_ref, e_ref, gb_ref, o_ref, kbf, wbf, bias_ref):
        qi = pl.program_id(1)

        @pl.when(qi == 0)
        def _():
            kbf[...] = kv_ref[...].astype(bf16)
            wbf[:, 0:WINDOW] = jnp.zeros((F, WINDOW), bf16)
            wbf[:, WINDOW:] = win_ref[...].astype(bf16)

        q0 = pl.multiple_of(qi * tq, tq)
        qs = q_ref[...] * scale
        gates = jax.nn.sigmoid(sm_ref[...] + gb_ref[...])
        qpos_c = q0 + lax.broadcasted_iota(jnp.int32, (tq, 1), 0)
        qpos_r = q0 + lax.broadcasted_iota(jnp.int32, (1, tq), 1)
        qpos_c4 = jnp.concatenate([qpos_c] * hpg, axis=0)
        qpos_r4 = jnp.concatenate([qpos_r] * hpg, axis=1)
        slot_r = _slot_block(lax.broadcasted_iota(jnp.int32, (1, LANES), 1))
        slot_c = _slot_block(lax.broadcasted_iota(jnp.int32, (LANES, 1), 0))
        col = lax.broadcasted_iota(jnp.int32, (1, span), 1)
        dist = (lax.broadcasted_iota(jnp.int32, (tq, 1), 0) + WINDOW) - col
        wbias = jnp.where((dist >= 0) & (dist < WINDOW) & (col + q0 >= WINDOW), 0.0, NEG)
        outs = []
        for g in range(N_KV):
            qg = jnp.concatenate([qs[:, (g * hpg + c) * hd:(g * hpg + c + 1) * hd] for c in range(hpg)], axis=0).astype(bf16)
            kct = poolt_ref[g * hd:(g + 1) * hd, :].astype(bf16)
            kc = pool_ref[:, g * hd:(g + 1) * hd].astype(bf16)
            vc = pool_ref[:, (N_KV + g) * hd:(N_KV + g + 1) * hd].astype(bf16)
            e_cmp, r_cmp = _masked_exp(_nn(qg, kct), (slot_r * CMP_BLOCK + CMP_BLOCK - 1) <= qpos_c4)
            o_cmp = _nn(e_cmp.astype(bf16), vc)
            kpos = lax.broadcasted_iota(jnp.int32, (1, L), 1)
            all_selected = q0 + tq <= TOP_N * SEL_BLOCK

            @pl.when(all_selected)
            def _():
                bias_ref[...] = jnp.where(kpos <= qpos_c, 0.0, NEG)

            @pl.when(jnp.logical_not(all_selected))
            def _():
                pt = _masked_softmax(_nt(kc, qg), (slot_c * CMP_BLOCK + CMP_BLOCK - 1) <= qpos_r4, axis=0)
                impt = pt[:, 0:tq]
                for c in range(1, hpg):
                    impt = impt + pt[:, c * tq:(c + 1) * tq]
                imp = impt[0:n_sel] + impt[LANES // 2:LANES // 2 + n_sel]
                blk = lax.broadcasted_iota(jnp.int32, (n_sel, 1), 0)
                cur = qpos_r // SEL_BLOCK
                valid = blk <= cur
                forced = (blk == 0) | (blk == cur) | (blk == cur - 1)
                score = jnp.where(valid, jnp.where(forced, FORCE_SCORE, imp), NEG)
                selt = _select_blocks(score, valid, blk, n_sel, axis=0)
                sel = jnp.concatenate([selt, jnp.zeros((LANES - n_sel, tq), f32)], axis=0).T
                expd = _nn(sel.astype(bf16), e_ref[...])
                bias_ref[...] = jnp.where((expd > 0.5) & (kpos <= qpos_c), 0.0, NEG)

            def sel_step(j, carry):
                m, l, acc = carry
                k0 = pl.multiple_of(j * SEL_TK, SEL_TK)
                kt = kbf[g * hd:(g + 1) * hd, pl.ds(k0, SEL_TK)]
                vt = kbf[(N_KV + g) * hd:(N_KV + g + 1) * hd, pl.ds(k0, SEL_TK)]
                s = _nn(qg, kt).reshape(hpg, tq, SEL_TK) + bias_ref[:, pl.ds(k0, SEL_TK)][None]
                s = s.reshape(hpg * tq, SEL_TK)
                m_new = jnp.maximum(m, jnp.max(s, axis=-1, keepdims=True))
                alpha = jnp.exp(m - m_new)
                pj = jnp.exp(s - m_new)
                l = alpha * l + jnp.sum(pj, axis=-1, keepdims=True)
                acc = alpha * acc + _nt(pj.astype(bf16), vt)
                return m_new, l, acc

            init = (jnp.full((hpg * tq, 1), NEG, f32), jnp.zeros((hpg * tq, 1), f32), jnp.zeros((hpg * tq, hd), f32))
            _, l_sel, o_sel = lax.fori_loop(0, (q0 + tq + SEL_TK - 1) // SEL_TK, sel_step, init)
            r_sel = 1.0 / l_sel
            kw_ = wbf[g * hd:(g + 1) * hd, pl.ds(q0, span)]
            vw_ = wbf[(N_KV + g) * hd:(N_KV + g + 1) * hd, pl.ds(q0, span)]
            s_win = (_nn(qg, kw_).reshape(hpg, tq, span) + wbias[None]).reshape(hpg * tq, span)
            e_win = jnp.exp(s_win - jnp.max(s_win, axis=-1, keepdims=True))
            r_win = 1.0 / jnp.sum(e_win, axis=-1, keepdims=True)
            o_win = _nt(e_win.astype(bf16), vw_)
            for c in range(hpg):
                h = g * hpg + c
                rows = slice(c * tq, (c + 1) * tq)
                gc, gs, gw = (gates[:, goff + k * H + h:goff + k * H + h + 1] for k in range(3))
                outs.append((gc * r_cmp[rows]) * o_cmp[rows] + (gs * r_sel[rows]) * o_sel[rows] + (gw * r_win[rows]) * o_win[rows])
        o_ref[...] = jnp.concatenate(outs, axis=1).astype(o_ref.dtype)

    tok = lambda w: pl.BlockSpec((tq, w), lambda b, i: (b * nq + i, 0))
    return pl.pallas_call(
        body, grid=(nb, nq),
        in_specs=[tok(dq), tok(LANES), pl.BlockSpec((None, F, L), lambda b, i: (b, 1, 0)),
                  pl.BlockSpec((None, F, L), lambda b, i: (b, 0, 0)),
                  pl.BlockSpec((None, LANES, F), lambda b, i: (b, 0, 0)), pl.BlockSpec((None, F, LANES), lambda b, i: (b, 0, 0)),
                  pl.BlockSpec((LANES, L), lambda b, i: (0, 0)), pl.BlockSpec((1, LANES), lambda b, i: (0, 0))],
        out_specs=tok(dq), out_shape=jax.ShapeDtypeStruct((T, dq), bf16),
        scratch_shapes=[pltpu.VMEM((F, L), bf16), pltpu.VMEM((F, L + WINDOW), bf16), pltpu.VMEM((tq, L), f32)],
        compiler_params=_params(("parallel", "arbitrary"), 48), name="nsa_prompt",
    )(q, small, kvt, wint, pool, poolt, expand, gate_b)


def _nsa_sample(q, kv_new, win_new, small, cache_t, win_t, page_table, layer, wt_page, seg, expand, gate_b, goff, dec_seq):
    T, dq = q.shape
    S, n_pages = page_table.shape
    page = cache_t.shape[-1]
    keep = win_t.shape[-1]
    wd = win_t.shape[2]
    past = n_pages * page
    hd = HEAD_DIM
    H = dq // hd
    hpg = H // N_KV
    F = 2 * N_KV * hd
    G2 = N_KV * hd
    Lt = dec_seq
    SB = SAMPLE_SEQS
    RT = SB * Lt
    RS = hpg * N_KV * Lt
    R = SB * RS
    scale = hd ** -0.5
    n_sel = past // SEL_BLOCK + 1
    new_blk = past // SEL_BLOCK
    k_start = past - keep
    assert N_KV == 2 and G2 == LANES and R == LANES and RT % 8 == 0 and S % SB == 0
    assert past % SEL_BLOCK == 0 and Lt <= SEL_BLOCK and n_sel <= LANES // 2
    assert page == LANES and keep % LANES == 0 and keep <= WINDOW and Lt <= LANES and wd == 2 * G2

    r_ = jnp.arange(R)
    seq_r, c_r, g_r, t_r = r_ // RS, (r_ // (N_KV * Lt)) % hpg, (r_ // Lt) % N_KV, r_ % Lt
    z_of_r = ((g_r * hpg + c_r) * SB + seq_r) * Lt + t_r
    perm = (z_of_r[:, None] == jnp.arange(R)[None, :]).astype(bf16)
    rep = jnp.stack([(jnp.arange(LANES)[:, None] == goff + k * H + jnp.arange(dq)[None, :] // hd) for k in range(3)]).astype(bf16)
    shift = (jnp.arange(Lt)[:, None] + (LANES - Lt) == jnp.arange(LANES)[None, :]).astype(bf16)

    def body(pt_ref, q_ref, kvn_ref, wn_ref, sm_ref, *refs):
        pages = [refs[sq * n_pages:(sq + 1) * n_pages] for sq in range(SB)]
        win_ref, wt_ref, seg_ref, e_ref, gb_ref, shift_ref, perm_ref, rep_ref, o_ref, wout_ref = refs[SB * n_pages:]
        qs = q_ref[...] * scale
        kvn = kvn_ref[...]
        wn = wn_ref[...]
        gates = jax.nn.sigmoid(sm_ref[...] + gb_ref[...])
        lane = lax.broadcasted_iota(jnp.int32, (1, LANES), 1)
        rowi = lax.broadcasted_iota(jnp.int32, (R, 1), 0)
        g_row = (rowi // Lt) % N_KV
        t_row = rowi % Lt
        qpos = past + t_row

        staged = []
        for g in range(N_KV):
            for c in range(hpg):
                pair = jnp.concatenate([qs[:, c * hd:(c + 1) * hd], qs[:, (hpg + c) * hd:(hpg + c + 1) * hd]], axis=1)
                staged.append(jnp.where((lane >= g * hd) & (lane < (g + 1) * hd), pair, 0.0))
        qbd32 = _nn(perm_ref[...], jnp.concatenate(staged, axis=0).astype(bf16))
        qbd = qbd32.astype(bf16)
        seq_rows = lambda a, sq: a[sq * RS:(sq + 1) * RS]
        per_seq = lambda fn: jnp.concatenate([fn(sq) for sq in range(SB)], axis=0)
        own_lanes = lambda o: jnp.where(g_row == 0, o, pltpu.roll(o, hd, 1))[:, 0:hd]

        poolt = []
        for sq in range(SB):
            hi, lo = _split2(jnp.concatenate([pages[sq][p][0:F, :] * wt_ref[...] for p in range(n_pages)], axis=1))
            poolt.append(_nn(hi, seg_ref[...]) + _nn(lo, seg_ref[...]))

        s_c = per_seq(lambda sq: _nn(seq_rows(qbd, sq), poolt[sq][0:G2].astype(bf16)))
        p = _masked_softmax(s_c, (_slot_block(lane) * CMP_BLOCK + CMP_BLOCK - 1) <= qpos)
        p_bf = p.astype(bf16)
        o_cmp = own_lanes(per_seq(lambda sq: _nt(seq_rows(p_bf, sq), poolt[sq][G2:2 * G2].astype(bf16))))

        def imp_seq(sq):
            ps = seq_rows(p, sq)
            acc = ps[0:N_KV * Lt]
            for c in range(1, hpg):
                acc = acc + ps[c * N_KV * Lt:(c + 1) * N_KV * Lt]
            return acc
        imp = per_seq(imp_seq)
        imp = imp + pltpu.roll(imp, LANES // 2, 1)
        nrow = SB * N_KV * Lt
        qpos_i = past + lax.broadcasted_iota(jnp.int32, (nrow, 1), 0) % Lt
        cur = qpos_i // SEL_BLOCK
        valid = (lane <= cur) & (lane < n_sel)
        forced = (lane == 0) | (lane == cur) | (lane == cur - 1)
        score = jnp.where(valid, jnp.where(forced, FORCE_SCORE, imp), NEG)
        sel = _select_blocks(score, valid, lane, n_sel, axis=1)
        expd = _nn(sel.astype(bf16), e_ref[...])
        kpos = lax.broadcasted_iota(jnp.int32, (1, past), 1)
        bias = jnp.where((expd > 0.5) & (kpos <= qpos_i), 0.0, NEG)
        gt = N_KV * Lt
        tile_c = lambda a: per_seq(lambda sq: jnp.concatenate([a[sq * gt:(sq + 1) * gt]] * hpg, axis=0))
        sel_new = tile_c(sel[:, new_blk:new_blk + 1]) > 0.5

        def new_rows(src, lo_lane):
            return [per_seq(lambda sq: jnp.broadcast_to(src[sq * Lt + t:sq * Lt + t + 1, lo_lane:lo_lane + G2], (RS, G2)))
                    for t in range(Lt)]

        def attend(s_past, k_new, v_new, ok_new, v_past_fn):
            s_new = [jnp.where(ok_new & (t_row >= t), jnp.sum(qbd32 * k_new[t], axis=-1, keepdims=True), NEG) for t in range(Lt)]
            m = jnp.max(s_past, axis=-1, keepdims=True)
            for s_t in s_new:
                m = jnp.maximum(m, s_t)
            e_past = jnp.exp(s_past - m)
            den = jnp.sum(e_past, axis=-1, keepdims=True)
            e_bf = e_past.astype(bf16)
            acc = per_seq(lambda sq: v_past_fn(sq, seq_rows(e_bf, sq)))
            for t, s_t in enumerate(s_new):
                e_t = jnp.exp(s_t - m)
                den = den + e_t
                acc = acc + e_t * v_new[t]
            return own_lanes(acc) * (1.0 / den)

        s_sel = per_seq(lambda sq: jnp.concatenate([_nn(seq_rows(qbd, sq), pages[sq][p][F:F + G2, :].astype(bf16))
                                                    for p in range(n_pages)], axis=1)) + tile_c(bias)

        def v_sel(sq, e):
            acc = _nt(e[:, 0:page], pages[sq][0][F + G2:F + 2 * G2, :].astype(bf16))
            for p in range(1, n_pages):
                acc += _nt(e[:, p * page:(p + 1) * page], pages[sq][p][F + G2:F + 2 * G2, :].astype(bf16))
            return acc

        o_sel = attend(s_sel, new_rows(kvn, F), new_rows(kvn, F + G2), sel_new, v_sel)
        dist = qpos - (k_start + lax.broadcasted_iota(jnp.int32, (1, keep), 1))
        s_win = jnp.where((dist >= 0) & (dist < WINDOW),
                          per_seq(lambda sq: _nn(seq_rows(qbd, sq), win_ref[sq, 0:G2, :].astype(bf16))), NEG)
        o_win = attend(s_win, new_rows(wn, 0), new_rows(wn, G2), jnp.full((R, 1), True),
                       lambda sq, e: _nt(e, win_ref[sq, G2:2 * G2, :].astype(bf16)))

        def token_major(o):
            hi, lo = _split2(o)
            st = _tn(perm_ref[...], hi) + _tn(perm_ref[...], lo)
            return jnp.concatenate([st[k * RT:(k + 1) * RT] for k in range(N_KV * hpg)], axis=1)

        def gate(k):
            hi, lo = _split2(gates)
            return _nn(hi, rep_ref[k]) + _nn(lo, rep_ref[k])

        o_ref[...] = gate(0) * token_major(o_cmp) + gate(1) * token_major(o_sel) + gate(2) * token_major(o_win)

        for sq in range(SB):
            rolled = pltpu.roll(win_ref[sq], keep - Lt, 1)
            hi, mid, lo = _split3(wn[sq * Lt:(sq + 1) * Lt])
            new_t = _tn(hi, shift_ref[...]) + _tn(mid, shift_ref[...]) + _tn(lo, shift_ref[...])
            last = jnp.where(lane >= LANES - Lt, new_t, rolled[:, keep - LANES:keep])
            if keep > LANES:
                wout_ref[sq, :, 0:keep - LANES] = rolled[:, 0:keep - LANES]
            wout_ref[sq, :, keep - LANES:keep] = last

    tok = lambda w: pl.BlockSpec((RT, w), lambda i, pt: (i, 0))
    cst = lambda shp: pl.BlockSpec(shp, lambda i, pt: tuple(0 for _ in shp))
    page_spec = lambda sq, p: pl.BlockSpec((None, None, 2 * F, page), lambda i, pt: (layer, pt[i * SB + sq, p], 0, 0))
    grid_spec = pltpu.PrefetchScalarGridSpec(
        num_scalar_prefetch=1, grid=(S // SB,),
        in_specs=[tok(dq), tok(2 * F), tok(wd), tok(LANES)] + [page_spec(sq, p) for sq in range(SB) for p in range(n_pages)] + [
            pl.BlockSpec((None, SB, wd, keep), lambda i, pt: (layer, i, 0, 0)),
            cst((F, page)), cst((past, LANES)), cst((LANES, past)), cst((1, LANES)), cst((Lt, LANES)), cst((R, R)),
            cst((3, LANES, dq))],
        out_specs=[tok(dq), pl.BlockSpec((SB, wd, keep), lambda i, pt: (i, 0, 0))])
    return pl.pallas_call(
        body, grid_spec=grid_spec,
        out_shape=[jax.ShapeDtypeStruct((T, dq), f32), jax.ShapeDtypeStruct((S, wd, keep), f32)],
        compiler_params=_params(("parallel",), 56), name="nsa_sample",
    )(page_table, q, kv_new, win_new, small, *([cache_t] * (SB * n_pages)), win_t, wt_page, seg, expand, gate_b, shift, perm, rep)


def kernel(x_prompt, x_sample, mem_prompt, cache_nsa_kv, cache_mem_kv, state_nsa_win, state_ssm, state_conv, page_table, norm_mix_pre, w_in, conv_w, conv_b, dt_bias, a_log, d_skip, ssm_norm, cmp_wk, cmp_wv, gate_b, w_out, norm_mix_post, norm_mem_src, w_mem_q, w_mem_k, w_mem_v, w_mem_o, norm_mem_pre, norm_mem_post, norm_ffn_pre, w_up, w_down, norm_ffn_post):
    B, L, D = x_prompt.shape
    S, Lt, _ = x_sample.shape
    depth = w_in.shape[0]
    M = mem_prompt.shape[1]
    d_ssm = ssm_norm.shape[1]
    C = conv_w.shape[2]
    H = dt_bias.shape[1]
    P, N = state_ssm.shape[3], state_ssm.shape[4]
    d_att = w_out.shape[1] - d_ssm
    Ha = d_att // HEAD_DIM
    n_pool, page = cache_nsa_kv.shape[1], cache_nsa_kv.shape[2]
    keep = state_nsa_win.shape[2]
    kvw = 4 * N_KV * HEAD_DIM
    winw = 2 * N_KV * HEAD_DIM
    past = page_table.shape[1] * page
    assert H + 3 * Ha <= LANES

    cache_t = cache_nsa_kv.transpose(0, 1, 3, 4, 5, 2).reshape(depth, n_pool, kvw, page)
    win_t = state_nsa_win.transpose(0, 1, 3, 4, 5, 2).reshape(depth, S, winw, keep)
    ssm_t = state_ssm.transpose(0, 2, 3, 4, 1).reshape(depth, H, P * N, S)
    conv_t = state_conv.transpose(0, 2, 1, 3)

    o1 = d_ssm
    o2 = o1 + C
    o3 = o2 + H
    o4 = o3 + d_att
    o5 = o4 + kvw
    o6 = o5 + winw
    n_gate = 3 * Ha

    expand_p = _expand_const(L)
    expand_s = _expand_const(past)
    xp = x_prompt.reshape(B * L, D)
    xs = x_sample.reshape(S * Lt, D)
    mem = mem_prompt.reshape(B * M, D)
    kv_p, kv_s, win_p, win_s, ssm_p, ssm_s, conv_p, conv_s, mkv_out = [], [], [], [], [], [], [], [], []
    for l in range(depth):
        wl = w_in[l].astype(bf16)
        w_z, w_xbc, w_q = wl[:, :o1], wl[:, o1:o2], wl[:, o3:o4]
        w_kv, w_win = wl[:, o4:o5], wl[:, o5:o6]
        w_small = jnp.concatenate([wl[:, o2:o3], wl[:, o6:o6 + n_gate], jnp.zeros((D, LANES - H - n_gate), bf16)], axis=1)
        gb = jnp.zeros((1, LANES), f32).at[0, H:H + n_gate].set(gate_b[l])
        wo = w_out[l].astype(bf16)
        wt_full, seg_full = _pool_consts(cmp_wk[l], cmp_wv[l], L)
        wt_page, seg_page = _pool_consts(cmp_wk[l], cmp_wv[l], past)

        mkv, = _norm_proj(mem, norm_mem_src[l], [jnp.concatenate([w_mem_k[l], w_mem_v[l]], axis=1).astype(bf16)])
        mkv_out.append(mkv.reshape(B, M, 2, MEM_HEADS, D // MEM_HEADS))

        z, xbc, q, small, kvt, wint = _norm_proj(xp, norm_mix_pre[l], [w_z, w_xbc, w_q, w_small], [w_kv.T, w_win.T], seq=L,
                                                 row_dtypes=[f32, f32, bf16, f32])
        y_ssd, st_new, cv_new = _ssd_prompt(xbc, z, small, conv_w[l], conv_b[l], dt_bias[l], a_log[l], d_skip[l], ssm_norm[l], B)
        pool, poolt = _cmp_pool_prompt(kvt, wt_full, seg_full)
        o_att = _nsa_prompt(q, small, kvt, wint, pool, poolt, expand_p, gb, H)
        xp = _linear_post(xp, [y_ssd, o_att], [wo[:d_ssm], wo[d_ssm:]], norm_mix_post[l])
        kv_p.append(kvt)
        win_p.append(wint[:, :, L - min(WINDOW, L):])
        ssm_p.append(st_new)
        conv_p.append(cv_new)

        z, xbc, q, small, kvr, winr = _norm_proj(xs, norm_mix_pre[l], [w_z, w_xbc, w_q, w_small, w_kv, w_win])
        y_ssd, st_new, cv_new = _ssd_sample(xbc, z, small, ssm_t, conv_t, l, conv_w[l], conv_b[l], dt_bias[l], a_log[l],
                                            d_skip[l], ssm_norm[l], Lt)
        o_att, wnew = _nsa_sample(q, kvr, winr, small, cache_t, win_t, page_table, l, wt_page[:, :page], seg_page,
                                  expand_s, gb, H, Lt)
        xs = _linear_post(xs, [y_ssd, o_att], [wo[:d_ssm], wo[d_ssm:]], norm_mix_post[l])
        kv_s.append(kvr)
        win_s.append(wnew)
        ssm_s.append(st_new)
        conv_s.append(cv_new)

        wq, wmo = w_mem_q[l].astype(bf16), w_mem_o[l].astype(bf16)
        qm, = _norm_proj(xp, norm_mem_pre[l], [wq], row_dtypes=[bf16])
        xp = _linear_post(xp, [_mem_attn_prompt(qm, mkv, L)], [wmo], norm_mem_post[l])
        qm, = _norm_proj(xs, norm_mem_pre[l], [wq])
        xs = _linear_post(xs, [_mem_attn_sample(qm, cache_mem_kv, l, Lt)], [wmo], norm_mem_post[l])

        wu, wd = w_up[l].astype(bf16), w_down[l].astype(bf16)
        xp = _ffn(xp, norm_ffn_pre[l], wu, wd, norm_ffn_post[l], tm=1024)
        xs = _ffn(xs, norm_ffn_pre[l], wu, wd, norm_ffn_post[l], tm=512)

    kv5 = (4, N_KV, HEAD_DIM)
    w5 = (2, N_KV, HEAD_DIM)
    return (xp.reshape(B, L, D), xs.reshape(S, Lt, D),
            jnp.stack(kv_p).reshape(depth, B, *kv5, L).transpose(0, 1, 5, 2, 3, 4),
            jnp.stack(kv_s).reshape(depth, S, Lt, *kv5),
            jnp.stack(win_p).reshape(depth, B, *w5, min(WINDOW, L)).transpose(0, 1, 5, 2, 3, 4),
            jnp.stack(win_s).reshape(depth, S, *w5, keep).transpose(0, 1, 5, 2, 3, 4),
            jnp.stack(ssm_p), jnp.stack(ssm_s).reshape(depth, H, P, N, S).transpose(0, 4, 1, 2, 3),
            jnp.stack(conv_p), jnp.stack(conv_s).transpose(0, 2, 1, 3),
            jnp.stack(mkv_out))
```

```python
import functools

import jax
import jax.numpy as jnp
from jax import lax
from jax.experimental import pallas as pl
from jax.experimental.pallas import tpu as pltpu

f32 = jnp.float32
bf16 = jnp.bfloat16
NEG = float("-inf")

SSM_HEAD_DIM = 64
SSM_GROUPS = 2
SSM_STATE = 64
SSD_CHUNK = 128
HEAD_DIM = 64
N_KV = 2
CMP_BLOCK = 32
SEL_BLOCK = 64
TOP_N = 16
WINDOW = 512
FORCE_SCORE = 1.0e4
MASK_BIG = 1.0e30
MEM_HEADS = 4
EPS = 1e-6

LANES = 128
V7X_VMEM_BYTES = 64 * 1024 * 1024
SEL_TK = 512
SAMPLE_SEQS = 4
ROW_CHUNK = 256


def _params(sem, vmem_mb):
    assert vmem_mb * 1024 * 1024 < V7X_VMEM_BYTES
    return pltpu.CompilerParams(dimension_semantics=sem, vmem_limit_bytes=vmem_mb * 1024 * 1024)


def _nn(a, b):
    return jnp.dot(a, b, preferred_element_type=f32)


def _nt(a, b):
    return lax.dot_general(a, b, (((1,), (1,)), ((), ())), preferred_element_type=f32)


def _tn(a, b):
    return lax.dot_general(a, b, (((0,), (0,)), ((), ())), preferred_element_type=f32)


def _rms(x, g):
    return x * lax.rsqrt(jnp.mean(x * x, axis=-1, keepdims=True) + EPS) * g


def _masked_exp(s, mask, axis=-1):
    s = jnp.where(mask, s, NEG)
    m = jnp.max(s, axis=axis, keepdims=True)
    m = jnp.where(jnp.isfinite(m), m, 0.0)
    e = jnp.exp(s - m)
    d = jnp.sum(e, axis=axis, keepdims=True)
    return e, 1.0 / jnp.where(d > 0, d, 1.0)


def _masked_softmax(s, mask, axis=-1):
    e, r = _masked_exp(s, mask, axis)
    return e * r


def _split2(x):
    hi = x.astype(bf16)
    lo = (x - hi.astype(f32)).astype(bf16)
    return hi, lo


def _split3(x):
    hi = x.astype(bf16)
    r = x - hi.astype(f32)
    mid = r.astype(bf16)
    lo = (r - mid.astype(f32)).astype(bf16)
    return hi, mid, lo


def _norm_proj(x, g, ws, wts=(), seq=None, tm=512, row_dtypes=None):
    T, D = x.shape
    tm = min(tm, T)
    assert T % tm == 0
    ws, wts = list(ws), list(wts)
    n_row = len(ws)
    tps = (seq // tm) if wts else 1

    def body(x_ref, g_ref, *refs):
        w_refs, o_refs = refs[:len(ws) + len(wts)], refs[len(ws) + len(wts):]
        for r0 in range(0, tm, ROW_CHUNK):
            rows = slice(r0, min(r0 + ROW_CHUNK, tm))
            xn = _rms(x_ref[rows, :], g_ref[...]).astype(bf16)
            for k in range(n_row):
                o_refs[k][rows, :] = _nn(xn, w_refs[k][...]).astype(o_refs[k].dtype)
            for k in range(n_row, len(w_refs)):
                o_refs[k][:, rows] = _nt(w_refs[k][...], xn)

    in_specs = [pl.BlockSpec((tm, D), lambda i: (i, 0)), pl.BlockSpec((1, D), lambda i: (0, 0))]
    in_specs += [pl.BlockSpec(w.shape, lambda i: (0, 0)) for w in ws + wts]
    out_specs = [pl.BlockSpec((tm, w.shape[1]), lambda i: (i, 0)) for w in ws]
    out_specs += [pl.BlockSpec((None, w.shape[0], tm), lambda i: (i // tps, 0, i % tps)) for w in wts]
    row_dtypes = row_dtypes or [f32] * n_row
    out_shape = [jax.ShapeDtypeStruct((T, w.shape[1]), dt) for w, dt in zip(ws, row_dtypes)]
    out_shape += [jax.ShapeDtypeStruct((T // seq, w.shape[0], seq), f32) for w in wts]
    return pl.pallas_call(body, grid=(T // tm,), in_specs=in_specs, out_specs=out_specs, out_shape=out_shape,
                          compiler_params=_params(("parallel",), 48), name="norm_proj")(x, g.reshape(1, D), *ws, *wts)


def _linear_post(res, acts, ws, g, tm=512):
    T, D = res.shape
    tm = min(tm, T)
    n = len(acts)

    def body(*refs):
        a_refs, w_refs = refs[:n], refs[n:2 * n]
        g_ref, res_ref, o_ref = refs[2 * n:]
        for r0 in range(0, tm, ROW_CHUNK):
            rows = slice(r0, min(r0 + ROW_CHUNK, tm))
            acc = _nn(a_refs[0][rows, :].astype(bf16), w_refs[0][...])
            for k in range(1, n):
                acc += _nn(a_refs[k][rows, :].astype(bf16), w_refs[k][...])
            o_ref[rows, :] = res_ref[rows, :] + _rms(acc, g_ref[...])

    in_specs = [pl.BlockSpec((tm, a.shape[1]), lambda i: (i, 0)) for a in acts]
    in_specs += [pl.BlockSpec(w.shape, lambda i: (0, 0)) for w in ws]
    in_specs += [pl.BlockSpec((1, D), lambda i: (0, 0)), pl.BlockSpec((tm, D), lambda i: (i, 0))]
    return pl.pallas_call(body, grid=(T // tm,), in_specs=in_specs, out_specs=pl.BlockSpec((tm, D), lambda i: (i, 0)),
                          out_shape=jax.ShapeDtypeStruct((T, D), f32),
                          compiler_params=_params(("parallel",), 40), name="linear_post")(*acts, *ws, g.reshape(1, D), res)


def _ffn(x, g1, wu, wd, g2, tm, tc=2048):
    T, D = x.shape
    F = wu.shape[1]
    tm = min(tm, T)
    nj = F // tc

    def body(x_ref, g1_ref, wu_ref, wd_ref, g2_ref, o_ref, xn_s, acc_s):
        j = pl.program_id(1)

        @pl.when(j == 0)
        def _():
            xn_s[...] = _rms(x_ref[...], g1_ref[...]).astype(bf16)
            acc_s[...] = jnp.zeros_like(acc_s)

        for r0 in range(0, tm, ROW_CHUNK):
            rows = slice(r0, min(r0 + ROW_CHUNK, tm))
            h = _nn(xn_s[rows, :], wu_ref[...])
            h = jnp.square(jnp.maximum(h, 0.0))
            acc_s[rows, :] += _nn(h.astype(bf16), wd_ref[...])

        @pl.when(j == nj - 1)
        def _():
            o_ref[...] = x_ref[...] + _rms(acc_s[...], g2_ref[...])

    return pl.pallas_call(
        body, grid=(T // tm, nj),
        in_specs=[pl.BlockSpec((tm, D), lambda i, j: (i, 0)), pl.BlockSpec((1, D), lambda i, j: (0, 0)),
                  pl.BlockSpec((D, tc), lambda i, j: (0, j)), pl.BlockSpec((tc, D), lambda i, j: (j, 0)),
                  pl.BlockSpec((1, D), lambda i, j: (0, 0))],
        out_specs=pl.BlockSpec((tm, D), lambda i, j: (i, 0)),
        out_shape=jax.ShapeDtypeStruct((T, D), f32),
        scratch_shapes=[pltpu.VMEM((tm, D), bf16), pltpu.VMEM((tm, D), f32)],
        compiler_params=_params(("parallel", "arbitrary"), 48), name="ffn")(x, g1.reshape(1, D), wu, wd, g2.reshape(1, D))


def _mem_attn_prompt(q, mkv, seq, tq=512):
    T, D = q.shape
    M = mkv.shape[0] // (T // seq)
    hd = D // MEM_HEADS
    scale = hd ** -0.5
    tps = seq // tq

    def body(q_ref, kv_ref, o_ref):
        for h in range(MEM_HEADS):
            qh = (q_ref[:, h * hd:(h + 1) * hd] * scale).astype(bf16)
            kh = kv_ref[:, h * hd:(h + 1) * hd].astype(bf16)
            vh = kv_ref[:, D + h * hd:D + (h + 1) * hd].astype(bf16)
            s = _nt(qh, kh)
            e = jnp.exp(s - jnp.max(s, axis=-1, keepdims=True))
            d = jnp.sum(e, axis=-1, keepdims=True)
            o_ref[:, h * hd:(h + 1) * hd] = (_nn(e.astype(bf16), vh) * (1.0 / d)).astype(o_ref.dtype)

    return pl.pallas_call(
        body, grid=(T // tq,),
        in_specs=[pl.BlockSpec((tq, D), lambda i: (i, 0)), pl.BlockSpec((M, 2 * D), lambda i: (i // tps, 0))],
        out_specs=pl.BlockSpec((tq, D), lambda i: (i, 0)), out_shape=jax.ShapeDtypeStruct((T, D), bf16),
        compiler_params=_params(("parallel",), 40), name="mem_attn_prompt")(q, mkv)


def _mem_attn_sample(q, cache, layer, dec_seq):
    T, D = q.shape
    _, S, M, _, H, hd = cache.shape
    scale = hd ** -0.5
    spb = 8 // dec_seq
    assert 8 % dec_seq == 0 and S % spb == 0
    ncp = spb * 2 * H

    def body(q_ref, c_ref, o_ref, buf, sem):
        i = pl.program_id(0)
        n = pl.num_programs(0)

        def copies(step, slot):
            return [pltpu.make_async_copy(c_ref.at[layer, step * spb + s, :, kv, h, :], buf.at[slot, (s * 2 + kv) * H + h],
                                          sem.at[slot, (s * 2 + kv) * H + h])
                    for s in range(spb) for kv in range(2) for h in range(H)]

        @pl.when(i == 0)
        def _():
            for cp in copies(0, 0):
                cp.start()

        @pl.when(i + 1 < n)
        def _():
            for cp in copies(i + 1, (i + 1) % 2):
                cp.start()

        slot = i % 2
        for cp in copies(i, slot):
            cp.wait()
        pairs = [(s, h) for s in range(spb) for h in range(H)]
        qsc = (q_ref[...] * scale).astype(bf16)
        sc = jnp.concatenate([_nt(qsc[s * dec_seq:(s + 1) * dec_seq, h * hd:(h + 1) * hd], buf[slot, (s * 2) * H + h].astype(bf16))
                              for s, h in pairs], axis=0)
        e = jnp.exp(sc - jnp.max(sc, axis=-1, keepdims=True))
        r = 1.0 / jnp.sum(e, axis=-1, keepdims=True)
        for k, (s, h) in enumerate(pairs):
            rows = slice(k * dec_seq, (k + 1) * dec_seq)
            o_ref[s * dec_seq:(s + 1) * dec_seq, h * hd:(h + 1) * hd] = (
                _nn(e[rows].astype(bf16), buf[slot, (s * 2 + 1) * H + h].astype(bf16)) * r[rows])

    return pl.pallas_call(
        body, grid=(S // spb,),
        in_specs=[pl.BlockSpec((8, D), lambda i: (i, 0)), pl.BlockSpec(memory_space=pl.ANY)],
        out_specs=pl.BlockSpec((8, D), lambda i: (i, 0)), out_shape=jax.ShapeDtypeStruct((T, D), f32),
        scratch_shapes=[pltpu.VMEM((2, ncp, M, hd), f32), pltpu.SemaphoreType.DMA((2, ncp))],
        compiler_params=_params(("arbitrary",), 40), name="mem_attn_sample")(q, cache)


def _cumsum_steps(n):
    k, out = 1, []
    while k < n:
        out.append(k)
        k *= 2
    return out


def _ssd_prompt(xbc, z, small, conv_w, conv_b, dt_bias, a_log, d_skip, ssm_norm, nb):
    T, C = xbc.shape
    d_ssm = z.shape[1]
    H = d_ssm // SSM_HEAD_DIM
    P, N, L = SSM_HEAD_DIM, SSM_STATE, SSD_CHUNK
    hpg = H // SSM_GROUPS
    kw = conv_w.shape[0]
    seq = T // nb
    nc = seq // L
    assert seq % L == 0 and kw - 1 <= 8

    pad = lambda v: jnp.zeros((1, LANES), f32).at[0, :H].set(v)
    col = lambda v: jnp.broadcast_to(v[:, None], (H, LANES))

    rep = (jnp.arange(LANES)[:, None] == jnp.arange(d_ssm)[None, :] // P).astype(bf16)

    def body(xbc_ref, z_ref, sm_ref, cw_ref, cb_ref, dtb_r, alog_r, dskx_ref, dtb_c, alog_c, nrm_ref, rep_ref,
             y_ref, st_ref, cv_ref, xpad, hst):
        c = pl.program_id(1)

        @pl.when(c == 0)
        def _():
            xpad[0:8] = jnp.zeros((8, C), f32)
            hst[...] = jnp.zeros_like(hst)

        xpad[8:8 + L] = xbc_ref[...]
        acc = cb_ref[...] + cw_ref[0:1] * xpad[9 - kw:9 - kw + L]
        for k in range(1, kw):
            acc += cw_ref[k:k + 1] * xpad[9 - kw + k:9 - kw + k + L]
        xc = jax.nn.silu(acc)
        xpad[0:8] = xpad[L:L + 8]

        @pl.when(c == nc - 1)
        def _():
            cv_ref[...] = xpad[9 - kw:8]

        sm = sm_ref[...]
        dt_c = jax.nn.softplus(sm + dtb_r[...])
        acs_c = dt_c * (-jnp.exp(alog_r[...]))
        row = lax.broadcasted_iota(jnp.int32, (L, LANES), 0)
        for k in _cumsum_steps(L):
            acs_c = acs_c + jnp.where(row >= k, pltpu.roll(acs_c, k, 0), 0.0)
        dt_r = jax.nn.softplus(sm.T[0:H] + dtb_c[...])
        acs_r = dt_r * (-jnp.exp(alog_c[...]))
        lane = lax.broadcasted_iota(jnp.int32, (H, L), 1)
        for k in _cumsum_steps(L):
            acs_r = acs_r + jnp.where(lane >= k, pltpu.roll(acs_r, k, 1), 0.0)

        def per_head_lanes(v):
            hi, lo = _split2(v)
            return _nn(hi, rep_ref[...]) + _nn(lo, rep_ref[...])

        alast_c = acs_c[L - 1:L, :]
        xs_all = xc[:, 0:d_ssm]
        xdt_all = xs_all * per_head_lanes(dt_c)
        xd_all = (xdt_all * per_head_lanes(jnp.exp(alast_c - acs_c))).astype(bf16)
        xdt_bf = xdt_all.astype(bf16)
        causal = lax.broadcasted_iota(jnp.int32, (L, L), 0) >= lax.broadcasted_iota(jnp.int32, (L, L), 1)
        ydiags, yoffs = [], []
        for g in range(SSM_GROUPS):
            bg = xc[:, d_ssm + g * N:d_ssm + (g + 1) * N].astype(bf16)
            cg = xc[:, d_ssm + (SSM_GROUPS + g) * N:d_ssm + (SSM_GROUPS + g + 1) * N].astype(bf16)
            cb = _nt(cg, bg)
            hg = hst[g * hpg:(g + 1) * hpg].reshape(hpg * P, N)
            yoffs.append(_nt(cg, hg.astype(bf16)))
            for c4 in range(hpg):
                h = g * hpg + c4
                dec = jnp.exp(jnp.where(causal, acs_c[:, h:h + 1] - acs_r[h:h + 1, :], NEG))
                ydiags.append(_nn((cb * dec).astype(bf16), xdt_bf[:, h * P:(h + 1) * P]))
            dh = _tn(xd_all[:, g * hpg * P:(g + 1) * hpg * P], bg)
            for c4 in range(hpg):
                h = g * hpg + c4
                hst[h] = hst[h] * jnp.exp(alast_c[:, h:h + 1]) + dh[c4 * P:(c4 + 1) * P]

        y = (jnp.concatenate(ydiags, axis=1) + jnp.concatenate(yoffs, axis=1) * per_head_lanes(jnp.exp(acs_c))
             + dskx_ref[...] * xs_all)
        y_ref[...] = _rms(y * jax.nn.silu(z_ref[...]), nrm_ref[...]).astype(y_ref.dtype)

        @pl.when(c == nc - 1)
        def _():
            st_ref[...] = hst[...]

    tok = lambda w: pl.BlockSpec((L, w), lambda b, c: (b * nc + c, 0))
    cst = lambda shp: pl.BlockSpec(shp, lambda b, c: tuple(0 for _ in shp))
    return pl.pallas_call(
        body, grid=(nb, nc),
        in_specs=[tok(C), tok(d_ssm), tok(LANES), cst((kw, C)), cst((1, C)), cst((1, LANES)), cst((1, LANES)),
                  cst((1, d_ssm)), cst((H, LANES)), cst((H, LANES)), cst((1, d_ssm)), cst((LANES, d_ssm))],
        out_specs=[tok(d_ssm), pl.BlockSpec((None, H, P, N), lambda b, c: (b, 0, 0, 0)),
                   pl.BlockSpec((None, kw - 1, C), lambda b, c: (b, 0, 0))],
        out_shape=[jax.ShapeDtypeStruct((T, d_ssm), bf16), jax.ShapeDtypeStruct((nb, H, P, N), f32),
                   jax.ShapeDtypeStruct((nb, kw - 1, C), f32)],
        scratch_shapes=[pltpu.VMEM((L + 8, C), f32), pltpu.VMEM((H, P, N), f32)],
        compiler_params=_params(("parallel", "arbitrary"), 32), name="ssd_prompt",
    )(xbc, z, small, conv_w, conv_b.reshape(1, C), pad(dt_bias), pad(a_log), jnp.repeat(d_skip, P).reshape(1, d_ssm),
      col(dt_bias), col(a_log), ssm_norm.reshape(1, d_ssm), rep)


def _ssd_sample(xbc, z, small, state, conv_prev, layer, conv_w, conv_b, dt_bias, a_log, d_skip, ssm_norm, dec_seq):
    T, C = xbc.shape
    S = T // dec_seq
    d_ssm = z.shape[1]
    H = d_ssm // SSM_HEAD_DIM
    P, N = SSM_HEAD_DIM, SSM_STATE
    hpg = H // SSM_GROUPS
    kw = conv_w.shape[0]
    Lt = dec_seq
    tmaj = lambda a: a.reshape(S, Lt, a.shape[1]).transpose(1, 0, 2)

    def conv_body(x_ref, prev_ref, sm_ref, cw_ref, cb_ref, xt_ref, smt_ref, cv_ref):
        rows = [prev_ref[k] for k in range(kw - 1)] + [x_ref[t] for t in range(Lt)]
        for t in range(Lt):
            acc = cb_ref[...] + cw_ref[0:1] * rows[t]
            for k in range(1, kw):
                acc += cw_ref[k:k + 1] * rows[t + k]
            xt_ref[t] = jax.nn.silu(acc).T
            smt_ref[t] = sm_ref[t].T
        for k in range(kw - 1):
            cv_ref[k] = rows[Lt + k]

    xt, smt, conv_new = pl.pallas_call(
        conv_body, grid=(1,),
        in_specs=[pl.BlockSpec((Lt, S, C), lambda i: (0, 0, 0)), pl.BlockSpec((None, kw - 1, S, C), lambda i: (layer, 0, 0, 0)),
                  pl.BlockSpec((Lt, S, LANES), lambda i: (0, 0, 0)), pl.BlockSpec((kw, C), lambda i: (0, 0)),
                  pl.BlockSpec((1, C), lambda i: (0, 0))],
        out_specs=[pl.BlockSpec((Lt, C, S), lambda i: (0, 0, 0)), pl.BlockSpec((Lt, LANES, S), lambda i: (0, 0, 0)),
                   pl.BlockSpec((kw - 1, S, C), lambda i: (0, 0, 0))],
        out_shape=[jax.ShapeDtypeStruct((Lt, C, S), f32), jax.ShapeDtypeStruct((Lt, LANES, S), f32),
                   jax.ShapeDtypeStruct((kw - 1, S, C), f32)],
        compiler_params=_params(("arbitrary",), 32), name="ssd_sample_conv",
    )(tmaj(xbc), conv_prev, tmaj(small), conv_w, conv_b.reshape(1, C))

    def state_body(st_ref, xt_ref, smt_ref, dtb_ref, alog_ref, dsk_ref, nst_ref, y_ref, xdt_s):
        h = pl.program_id(0)
        g = h // hpg
        xoff = pl.multiple_of(h * P, P)
        boff = pl.multiple_of(d_ssm + g * N, N)
        coff = pl.multiple_of(d_ssm + (SSM_GROUPS + g) * N, N)
        a = -jnp.exp(alog_ref[pl.ds(h, 1), :])
        dsk = dsk_ref[pl.ds(h, 1), :]
        decs, bs, cs = [], [], []
        for t in range(Lt):
            dt = jax.nn.softplus(smt_ref[t, pl.ds(h, 1), :] + dtb_ref[pl.ds(h, 1), :])
            x = xt_ref[t, pl.ds(xoff, P), :]
            decs.append(jnp.exp(dt * a))
            xdt_s[t] = x * dt
            bs.append(xt_ref[t, pl.ds(boff, N), :])
            cs.append(xt_ref[t, pl.ds(coff, N), :])
            y_ref[t] = dsk * x

        def step(p, carry):
            r0 = pl.multiple_of(p * N, N)
            hp = st_ref[pl.ds(r0, N), :]
            for t in range(Lt):
                hp = hp * decs[t] + xdt_s[t, pl.ds(p, 1), :] * bs[t]
                y_ref[t, pl.ds(p, 1), :] += jnp.sum(cs[t] * hp, axis=0, keepdims=True)
            nst_ref[pl.ds(r0, N), :] = hp
            return carry

        lax.fori_loop(0, P, step, 0)

    colS = lambda v: jnp.broadcast_to(v[:, None], (H, S))
    new_state, yt = pl.pallas_call(
        state_body, grid=(H,),
        in_specs=[pl.BlockSpec((None, None, P * N, S), lambda h: (layer, h, 0, 0)),
                  pl.BlockSpec((Lt, C, S), lambda h: (0, 0, 0)), pl.BlockSpec((Lt, LANES, S), lambda h: (0, 0, 0)),
                  pl.BlockSpec((H, S), lambda h: (0, 0)), pl.BlockSpec((H, S), lambda h: (0, 0)),
                  pl.BlockSpec((H, S), lambda h: (0, 0))],
        out_specs=[pl.BlockSpec((None, P * N, S), lambda h: (h, 0, 0)), pl.BlockSpec((Lt, P, S), lambda h: (0, h, 0))],
        out_shape=[jax.ShapeDtypeStruct((H, P * N, S), f32), jax.ShapeDtypeStruct((Lt, d_ssm, S), f32)],
        scratch_shapes=[pltpu.VMEM((Lt, P, S), f32)],
        compiler_params=_params(("parallel",), 32), name="ssd_sample_state",
    )(state, xt, smt, colS(dt_bias), colS(a_log), colS(d_skip))

    def out_body(yt_ref, z_ref, nrm_ref, o_ref):
        for t in range(Lt):
            y = yt_ref[t].T * jax.nn.silu(z_ref[t])
            o_ref[t] = _rms(y, nrm_ref[...])

    y = pl.pallas_call(
        out_body, grid=(1,),
        in_specs=[pl.BlockSpec((Lt, d_ssm, S), lambda i: (0, 0, 0)), pl.BlockSpec((Lt, S, d_ssm), lambda i: (0, 0, 0)),
                  pl.BlockSpec((1, d_ssm), lambda i: (0, 0))],
        out_specs=pl.BlockSpec((Lt, S, d_ssm), lambda i: (0, 0, 0)),
        out_shape=jax.ShapeDtypeStruct((Lt, S, d_ssm), f32),
        compiler_params=_params(("arbitrary",), 32), name="ssd_sample_out",
    )(yt, tmaj(z), ssm_norm.reshape(1, d_ssm))
    return y.transpose(1, 0, 2).reshape(T, d_ssm), new_state, conv_new


def _slot_block(slot):
    return jnp.where(slot < LANES // 2, 2 * slot, 2 * (slot - LANES // 2) + 1)


def _pool_consts(cmp_wk, cmp_wv, rows):
    r = jnp.arange(rows)
    tile = lambda w: jnp.tile(w.T[:, r % CMP_BLOCK], (N_KV, 1))
    wt = jnp.concatenate([tile(cmp_wk), tile(cmp_wv)], axis=0)
    blk = r // CMP_BLOCK
    slot = jnp.where(blk % 2 == 0, blk // 2, LANES // 2 + blk // 2)
    seg = (slot[:, None] == jnp.arange(LANES)[None, :]).astype(bf16)
    return wt, seg


def _expand_const(n_keys):
    blk = jnp.arange(n_keys) // SEL_BLOCK
    return (jnp.arange(LANES)[:, None] == blk[None, :]).astype(bf16)


def _cmp_pool_prompt(kvt, wt, seg):
    nb, _, L = kvt.shape
    F = 2 * N_KV * HEAD_DIM

    def body(kv_ref, wt_ref, seg_ref, segt_ref, pool_ref, poolt_ref):
        hi, lo = _split2(kv_ref[...] * wt_ref[...])
        poolt_ref[...] = _nn(hi, seg_ref[...]) + _nn(lo, seg_ref[...])
        pool_ref[...] = _nt(segt_ref[...], hi) + _nt(segt_ref[...], lo)

    return pl.pallas_call(
        body, grid=(nb,),
        in_specs=[pl.BlockSpec((None, F, L), lambda b: (b, 0, 0)), pl.BlockSpec((F, L), lambda b: (0, 0)),
                  pl.BlockSpec((L, LANES), lambda b: (0, 0)), pl.BlockSpec((LANES, L), lambda b: (0, 0))],
        out_specs=[pl.BlockSpec((None, LANES, F), lambda b: (b, 0, 0)), pl.BlockSpec((None, F, LANES), lambda b: (b, 0, 0))],
        out_shape=[jax.ShapeDtypeStruct((nb, LANES, F), f32), jax.ShapeDtypeStruct((nb, F, LANES), f32)],
        compiler_params=_params(("parallel",), 32), name="cmp_pool_prompt")(kvt, wt, seg, seg.T)


def _select_blocks(score, valid, blk, n_blocks, axis):
    cnt = jnp.zeros(score.shape, f32)
    for i in range(n_blocks):
        si = score[i:i + 1, :] if axis == 0 else score[:, i:i + 1]
        beat = (si > score) | ((si == score) & (blk > i))
        cnt += jnp.where(beat, 1.0, 0.0)
    return jnp.where((cnt < TOP_N) & valid, 1.0, 0.0)


def _nsa_prompt(q, small, kvt, wint, pool, poolt, expand, gate_b, goff, tq=128):
    T, dq = q.shape
    nb, _, L = kvt.shape
    hd = HEAD_DIM
    H = dq // hd
    hpg = H // N_KV
    F = 2 * N_KV * hd
    nq = L // tq
    scale = hd ** -0.5
    n_sel = L // SEL_BLOCK
    span = tq + WINDOW
    assert L % SEL_TK == 0 and L % tq == 0 and tq % SEL_BLOCK == 0 and n_sel <= LANES // 2 and WINDOW % LANES == 0
    assert SEL_TK % tq == 0 and n_sel <= LANES - hd

    def body(q_ref, sm_ref, kv_ref, win_ref, pool_ref, poolt_ref, e_ref, gb_ref, o_ref, kaug, vbf, wbf, nsel_ref):
        qi = pl.program_id(1)

        @pl.when(qi == 0)
        def _():
            for g in range(N_KV):
                kaug[g * LANES:g * LANES + hd] = kv_ref[g * hd:(g + 1) * hd].astype(bf16)
                kaug[g * LANES + hd:(g + 1) * LANES] = e_ref[0:LANES - hd, :]
            vbf[...] = kv_ref[N_KV * hd:2 * N_KV * hd].astype(bf16)
            wbf[:, 0:WINDOW] = jnp.zeros((F, WINDOW), bf16)
            wbf[:, WINDOW:] = win_ref[...].astype(bf16)

        q0 = pl.multiple_of(qi * tq, tq)
        qs = q_ref[...] * scale
        gates = jax.nn.sigmoid(sm_ref[...] + gb_ref[...])
        qpos_c = q0 + lax.broadcasted_iota(jnp.int32, (tq, 1), 0)
        qpos_r = q0 + lax.broadcasted_iota(jnp.int32, (1, tq), 1)
        qpos_c4 = jnp.concatenate([qpos_c] * hpg, axis=0)
        qpos_r4 = jnp.concatenate([qpos_r] * hpg, axis=1)
        slot_r = _slot_block(lax.broadcasted_iota(jnp.int32, (1, LANES), 1))
        slot_c = _slot_block(lax.broadcasted_iota(jnp.int32, (LANES, 1), 0))
        col = lax.broadcasted_iota(jnp.int32, (1, span), 1)
        dist = (lax.broadcasted_iota(jnp.int32, (tq, 1), 0) + WINDOW) - col
        wbias = jnp.where((dist >= 0) & (dist < WINDOW) & (col + q0 >= WINDOW), 0.0, NEG)
        outs = []
        for g in range(N_KV):
            qg = jnp.concatenate([qs[:, (g * hpg + c) * hd:(g * hpg + c + 1) * hd] for c in range(hpg)], axis=0).astype(bf16)
            kct = poolt_ref[g * hd:(g + 1) * hd, :].astype(bf16)
            kc = pool_ref[:, g * hd:(g + 1) * hd].astype(bf16)
            vc = pool_ref[:, (N_KV + g) * hd:(N_KV + g + 1) * hd].astype(bf16)
            e_cmp, r_cmp = _masked_exp(_nn(qg, kct), (slot_r * CMP_BLOCK + CMP_BLOCK - 1) <= qpos_c4)
            o_cmp = _nn(e_cmp.astype(bf16), vc)
            all_selected = q0 + tq <= TOP_N * SEL_BLOCK

            @pl.when(all_selected)
            def _():
                nsel_ref[...] = jnp.zeros((tq, LANES), f32)

            @pl.when(jnp.logical_not(all_selected))
            def _():
                pt = _masked_softmax(_nt(kc, qg), (slot_c * CMP_BLOCK + CMP_BLOCK - 1) <= qpos_r4, axis=0)
                impt = pt[:, 0:tq]
                for c in range(1, hpg):
                    impt = impt + pt[:, c * tq:(c + 1) * tq]
                imp = impt[0:n_sel] + impt[LANES // 2:LANES // 2 + n_sel]
                blk = lax.broadcasted_iota(jnp.int32, (n_sel, 1), 0)
                cur = qpos_r // SEL_BLOCK
                valid = blk <= cur
                forced = (blk == 0) | (blk == cur) | (blk == cur - 1)
                score = jnp.where(valid, jnp.where(forced, FORCE_SCORE, imp), NEG)
                selt = _select_blocks(score, valid, blk, n_sel, axis=0)
                sel = jnp.concatenate([selt, jnp.zeros((LANES - n_sel, tq), f32)], axis=0).T
                nsel_ref[...] = jnp.where(sel > 0.5, 0.0, -MASK_BIG)
            qaug = jnp.concatenate([qg, jnp.concatenate([nsel_ref[:, 0:LANES - hd].astype(bf16)] * hpg, axis=0)], axis=1)

            def sel_step(j, carry, diagonal):
                m, l, acc = carry
                k0 = pl.multiple_of(j * SEL_TK, SEL_TK)
                kt = kaug[g * LANES:(g + 1) * LANES, pl.ds(k0, SEL_TK)]
                vt = vbf[g * hd:(g + 1) * hd, pl.ds(k0, SEL_TK)]
                s = _nn(qaug, kt)
                if diagonal:
                    s = jnp.where(k0 + lax.broadcasted_iota(jnp.int32, (1, SEL_TK), 1) <= qpos_c4, s, NEG)
                m_new = jnp.maximum(m, jnp.max(s, axis=-1, keepdims=True))
                alpha = jnp.exp(m - m_new)
                pj = jnp.exp(s - m_new)
                l = alpha * l + jnp.sum(pj, axis=-1, keepdims=True)
                acc = alpha * acc + _nt(pj.astype(bf16), vt)
                return m_new, l, acc

            init = (jnp.full((hpg * tq, 1), NEG, f32), jnp.zeros((hpg * tq, 1), f32), jnp.zeros((hpg * tq, hd), f32))
            last = (q0 + tq - 1) // SEL_TK
            carry = lax.fori_loop(0, last, functools.partial(sel_step, diagonal=False), init)
            _, l_sel, o_sel = sel_step(last, carry, diagonal=True)
            r_sel = 1.0 / l_sel
            kw_ = wbf[g * hd:(g + 1) * hd, pl.ds(q0, span)]
            vw_ = wbf[(N_KV + g) * hd:(N_KV + g + 1) * hd, pl.ds(q0, span)]
            s_win = (_nn(qg, kw_).reshape(hpg, tq, span) + wbias[None]).reshape(hpg * tq, span)
            e_win = jnp.exp(s_win - jnp.max(s_win, axis=-1, keepdims=True))
            r_win = 1.0 / jnp.sum(e_win, axis=-1, keepdims=True)
            o_win = _nt(e_win.astype(bf16), vw_)
            for c in range(hpg):
                h = g * hpg + c
                rows = slice(c * tq, (c + 1) * tq)
                gc, gs, gw = (gates[:, goff + k * H + h:goff + k * H + h + 1] for k in range(3))
                outs.append((gc * r_cmp[rows]) * o_cmp[rows] + (gs * r_sel[rows]) * o_sel[rows] + (gw * r_win[rows]) * o_win[rows])
        o_ref[...] = jnp.concatenate(outs, axis=1).astype(o_ref.dtype)

    tok = lambda w: pl.BlockSpec((tq, w), lambda b, i: (b * nq + i, 0))
    return pl.pallas_call(
        body, grid=(nb, nq),
        in_specs=[tok(dq), tok(LANES), pl.BlockSpec((None, F, L), lambda b, i: (b, 1, 0)),
                  pl.BlockSpec((None, F, L), lambda b, i: (b, 0, 0)),
                  pl.BlockSpec((None, LANES, F), lambda b, i: (b, 0, 0)), pl.BlockSpec((None, F, LANES), lambda b, i: (b, 0, 0)),
                  pl.BlockSpec((LANES, L), lambda b, i: (0, 0)), pl.BlockSpec((1, LANES), lambda b, i: (0, 0))],
        out_specs=tok(dq), out_shape=jax.ShapeDtypeStruct((T, dq), bf16),
        scratch_shapes=[pltpu.VMEM((N_KV * LANES, L), bf16), pltpu.VMEM((N_KV * hd, L), bf16), pltpu.VMEM((F, L + WINDOW), bf16),
                        pltpu.VMEM((tq, LANES), f32)],
        compiler_params=_params(("parallel", "arbitrary"), 48), name="nsa_prompt",
    )(q, small, kvt, wint, pool, poolt, expand, gate_b)


def _nsa_sample(q, kv_new, win_new, small, cache_t, win_t, page_table, layer, wt_page, seg, expand, gate_b, goff, dec_seq):
    T, dq = q.shape
    S, n_pages = page_table.shape
    page = cache_t.shape[-1]
    keep = win_t.shape[-1]
    wd = win_t.shape[2]
    past = n_pages * page
    hd = HEAD_DIM
    H = dq // hd
    hpg = H // N_KV
    F = 2 * N_KV * hd
    G2 = N_KV * hd
    Lt = dec_seq
    SB = SAMPLE_SEQS
    RT = SB * Lt
    RS = hpg * N_KV * Lt
    R = SB * RS
    scale = hd ** -0.5
    n_sel = past // SEL_BLOCK + 1
    new_blk = past // SEL_BLOCK
    k_start = past - keep
    assert N_KV == 2 and G2 == LANES and R == LANES and RT % 8 == 0 and S % SB == 0
    assert past % SEL_BLOCK == 0 and Lt <= SEL_BLOCK and n_sel <= LANES // 2
    assert page == LANES and keep % LANES == 0 and keep <= WINDOW and Lt <= LANES and wd == 2 * G2

    r_ = jnp.arange(R)
    seq_r, c_r, g_r, t_r = r_ // RS, (r_ // (N_KV * Lt)) % hpg, (r_ // Lt) % N_KV, r_ % Lt
    z_of_r = ((g_r * hpg + c_r) * SB + seq_r) * Lt + t_r
    perm = (z_of_r[:, None] == jnp.arange(R)[None, :]).astype(bf16)
    rep = jnp.stack([(jnp.arange(LANES)[:, None] == goff + k * H + jnp.arange(dq)[None, :] // hd) for k in range(3)]).astype(bf16)
    shift = (jnp.arange(Lt)[:, None] + (LANES - Lt) == jnp.arange(LANES)[None, :]).astype(bf16)

    def body(pt_ref, q_ref, kvn_ref, wn_ref, sm_ref, *refs):
        pages = [refs[sq * n_pages:(sq + 1) * n_pages] for sq in range(SB)]
        win_ref, wt_ref, seg_ref, e_ref, gb_ref, shift_ref, perm_ref, rep_ref, o_ref, wout_ref = refs[SB * n_pages:]
        qs = q_ref[...] * scale
        kvn = kvn_ref[...]
        wn = wn_ref[...]
        gates = jax.nn.sigmoid(sm_ref[...] + gb_ref[...])
        lane = lax.broadcasted_iota(jnp.int32, (1, LANES), 1)
        rowi = lax.broadcasted_iota(jnp.int32, (R, 1), 0)
        g_row = (rowi // Lt) % N_KV
        t_row = rowi % Lt
        qpos = past + t_row

        staged = []
        for g in range(N_KV):
            for c in range(hpg):
                pair = jnp.concatenate([qs[:, c * hd:(c + 1) * hd], qs[:, (hpg + c) * hd:(hpg + c + 1) * hd]], axis=1)
                staged.append(jnp.where((lane >= g * hd) & (lane < (g + 1) * hd), pair, 0.0))
        qbd32 = _nn(perm_ref[...], jnp.concatenate(staged, axis=0).astype(bf16))
        qbd = qbd32.astype(bf16)
        seq_rows = lambda a, sq: a[sq * RS:(sq + 1) * RS]
        per_seq = lambda fn: jnp.concatenate([fn(sq) for sq in range(SB)], axis=0)
        own_lanes = lambda o: jnp.where(g_row == 0, o, pltpu.roll(o, hd, 1))[:, 0:hd]

        poolt = []
        for sq in range(SB):
            hi, lo = _split2(jnp.concatenate([pages[sq][p][0:F, :] * wt_ref[...] for p in range(n_pages)], axis=1))
            poolt.append(_nn(hi, seg_ref[...]) + _nn(lo, seg_ref[...]))

        s_c = per_seq(lambda sq: _nn(seq_rows(qbd, sq), poolt[sq][0:G2].astype(bf16)))
        p = _masked_softmax(s_c, (_slot_block(lane) * CMP_BLOCK + CMP_BLOCK - 1) <= qpos)
        p_bf = p.astype(bf16)
        o_cmp = own_lanes(per_seq(lambda sq: _nt(seq_rows(p_bf, sq), poolt[sq][G2:2 * G2].astype(bf16))))

        def imp_seq(sq):
            ps = seq_rows(p, sq)
            acc = ps[0:N_KV * Lt]
            for c in range(1, hpg):
                acc = acc + ps[c * N_KV * Lt:(c + 1) * N_KV * Lt]
            return acc
        imp = per_seq(imp_seq)
        imp = imp + pltpu.roll(imp, LANES // 2, 1)
        nrow = SB * N_KV * Lt
        qpos_i = past + lax.broadcasted_iota(jnp.int32, (nrow, 1), 0) % Lt
        cur = qpos_i // SEL_BLOCK
        valid = (lane <= cur) & (lane < n_sel)
        forced = (lane == 0) | (lane == cur) | (lane == cur - 1)
        score = jnp.where(valid, jnp.where(forced, FORCE_SCORE, imp), NEG)
        sel = _select_blocks(score, valid, lane, n_sel, axis=1)
        expd = _nn(sel.astype(bf16), e_ref[...])
        kpos = lax.broadcasted_iota(jnp.int32, (1, past), 1)
        bias = jnp.where((expd > 0.5) & (kpos <= qpos_i), 0.0, NEG)
        gt = N_KV * Lt
        tile_c = lambda a: per_seq(lambda sq: jnp.concatenate([a[sq * gt:(sq + 1) * gt]] * hpg, axis=0))
        sel_new = tile_c(sel[:, new_blk:new_blk + 1]) > 0.5

        def new_rows(src, lo_lane):
            return [per_seq(lambda sq: jnp.broadcast_to(src[sq * Lt + t:sq * Lt + t + 1, lo_lane:lo_lane + G2], (RS, G2)))
                    for t in range(Lt)]

        def attend(s_past, k_new, v_new, ok_new, v_past_fn):
            s_new = [jnp.where(ok_new & (t_row >= t), jnp.sum(qbd32 * k_new[t], axis=-1, keepdims=True), NEG) for t in range(Lt)]
            m = jnp.max(s_past, axis=-1, keepdims=True)
            for s_t in s_new:
                m = jnp.maximum(m, s_t)
            e_past = jnp.exp(s_past - m)
            den = jnp.sum(e_past, axis=-1, keepdims=True)
            e_bf = e_past.astype(bf16)
            acc = per_seq(lambda sq: v_past_fn(sq, seq_rows(e_bf, sq)))
            for t, s_t in enumerate(s_new):
                e_t = jnp.exp(s_t - m)
                den = den + e_t
                acc = acc + e_t * v_new[t]
            return own_lanes(acc) * (1.0 / den)

        s_sel = per_seq(lambda sq: jnp.concatenate([_nn(seq_rows(qbd, sq), pages[sq][p][F:F + G2, :].astype(bf16))
                                                    for p in range(n_pages)], axis=1)) + tile_c(bias)

        def v_sel(sq, e):
            acc = _nt(e[:, 0:page], pages[sq][0][F + G2:F + 2 * G2, :].astype(bf16))
            for p in range(1, n_pages):
                acc += _nt(e[:, p * page:(p + 1) * page], pages[sq][p][F + G2:F + 2 * G2, :].astype(bf16))
            return acc

        o_sel = attend(s_sel, new_rows(kvn, F), new_rows(kvn, F + G2), sel_new, v_sel)
        dist = qpos - (k_start + lax.broadcasted_iota(jnp.int32, (1, keep), 1))
        s_win = jnp.where((dist >= 0) & (dist < WINDOW),
                          per_seq(lambda sq: _nn(seq_rows(qbd, sq), win_ref[sq, 0:G2, :].astype(bf16))), NEG)
        o_win = attend(s_win, new_rows(wn, 0), new_rows(wn, G2), jnp.full((R, 1), True),
                       lambda sq, e: _nt(e, win_ref[sq, G2:2 * G2, :].astype(bf16)))

        def token_major(o):
            hi, lo = _split2(o)
            st = _tn(perm_ref[...], hi) + _tn(perm_ref[...], lo)
            return jnp.concatenate([st[k * RT:(k + 1) * RT] for k in range(N_KV * hpg)], axis=1)

        def gate(k):
            hi, lo = _split2(gates)
            return _nn(hi, rep_ref[k]) + _nn(lo, rep_ref[k])

        o_ref[...] = gate(0) * token_major(o_cmp) + gate(1) * token_major(o_sel) + gate(2) * token_major(o_win)

        for sq in range(SB):
            rolled = pltpu.roll(win_ref[sq], keep - Lt, 1)
            hi, mid, lo = _split3(wn[sq * Lt:(sq + 1) * Lt])
            new_t = _tn(hi, shift_ref[...]) + _tn(mid, shift_ref[...]) + _tn(lo, shift_ref[...])
            last = jnp.where(lane >= LANES - Lt, new_t, rolled[:, keep - LANES:keep])
            if keep > LANES:
                wout_ref[sq, :, 0:keep - LANES] = rolled[:, 0:keep - LANES]
            wout_ref[sq, :, keep - LANES:keep] = last

    tok = lambda w: pl.BlockSpec((RT, w), lambda i, pt: (i, 0))
    cst = lambda shp: pl.BlockSpec(shp, lambda i, pt: tuple(0 for _ in shp))
    page_spec = lambda sq, p: pl.BlockSpec((None, None, 2 * F, page), lambda i, pt: (layer, pt[i * SB + sq, p], 0, 0))
    grid_spec = pltpu.PrefetchScalarGridSpec(
        num_scalar_prefetch=1, grid=(S // SB,),
        in_specs=[tok(dq), tok(2 * F), tok(wd), tok(LANES)] + [page_spec(sq, p) for sq in range(SB) for p in range(n_pages)] + [
            pl.BlockSpec((None, SB, wd, keep), lambda i, pt: (layer, i, 0, 0)),
            cst((F, page)), cst((past, LANES)), cst((LANES, past)), cst((1, LANES)), cst((Lt, LANES)), cst((R, R)),
            cst((3, LANES, dq))],
        out_specs=[tok(dq), pl.BlockSpec((SB, wd, keep), lambda i, pt: (i, 0, 0))])
    return pl.pallas_call(
        body, grid_spec=grid_spec,
        out_shape=[jax.ShapeDtypeStruct((T, dq), f32), jax.ShapeDtypeStruct((S, wd, keep), f32)],
        compiler_params=_params(("parallel",), 56), name="nsa_sample",
    )(page_table, q, kv_new, win_new, small, *([cache_t] * (SB * n_pages)), win_t, wt_page, seg, expand, gate_b, shift, perm, rep)


def kernel(x_prompt, x_sample, mem_prompt, cache_nsa_kv, cache_mem_kv, state_nsa_win, state_ssm, state_conv, page_table, norm_mix_pre, w_in, conv_w, conv_b, dt_bias, a_log, d_skip, ssm_norm, cmp_wk, cmp_wv, gate_b, w_out, norm_mix_post, norm_mem_src, w_mem_q, w_mem_k, w_mem_v, w_mem_o, norm_mem_pre, norm_mem_post, norm_ffn_pre, w_up, w_down, norm_ffn_post):
    B, L, D = x_prompt.shape
    S, Lt, _ = x_sample.shape
    depth = w_in.shape[0]
    M = mem_prompt.shape[1]
    d_ssm = ssm_norm.shape[1]
    C = conv_w.shape[2]
    H = dt_bias.shape[1]
    P, N = state_ssm.shape[3], state_ssm.shape[4]
    d_att = w_out.shape[1] - d_ssm
    Ha = d_att // HEAD_DIM
    n_pool, page = cache_nsa_kv.shape[1], cache_nsa_kv.shape[2]
    keep = state_nsa_win.shape[2]
    kvw = 4 * N_KV * HEAD_DIM
    winw = 2 * N_KV * HEAD_DIM
    past = page_table.shape[1] * page
    assert H + 3 * Ha <= LANES

    cache_t = cache_nsa_kv.transpose(0, 1, 3, 4, 5, 2).reshape(depth, n_pool, kvw, page)
    win_t = state_nsa_win.transpose(0, 1, 3, 4, 5, 2).reshape(depth, S, winw, keep)
    ssm_t = state_ssm.transpose(0, 2, 3, 4, 1).reshape(depth, H, P * N, S)
    conv_t = state_conv.transpose(0, 2, 1, 3)

    o1 = d_ssm
    o2 = o1 + C
    o3 = o2 + H
    o4 = o3 + d_att
    o5 = o4 + kvw
    o6 = o5 + winw
    n_gate = 3 * Ha

    expand_p = _expand_const(L)
    expand_s = _expand_const(past)
    xp = x_prompt.reshape(B * L, D)
    xs = x_sample.reshape(S * Lt, D)
    mem = mem_prompt.reshape(B * M, D)
    kv_p, kv_s, win_p, win_s, ssm_p, ssm_s, conv_p, conv_s, mkv_out = [], [], [], [], [], [], [], [], []
    for l in range(depth):
        wl = w_in[l].astype(bf16)
        w_z, w_xbc, w_q = wl[:, :o1], wl[:, o1:o2], wl[:, o3:o4]
        w_kv, w_win = wl[:, o4:o5], wl[:, o5:o6]
        w_small = jnp.concatenate([wl[:, o2:o3], wl[:, o6:o6 + n_gate], jnp.zeros((D, LANES - H - n_gate), bf16)], axis=1)
        gb = jnp.zeros((1, LANES), f32).at[0, H:H + n_gate].set(gate_b[l])
        wo = w_out[l].astype(bf16)
        wt_full, seg_full = _pool_consts(cmp_wk[l], cmp_wv[l], L)
        wt_page, seg_page = _pool_consts(cmp_wk[l], cmp_wv[l], past)

        mkv, = _norm_proj(mem, norm_mem_src[l], [jnp.concatenate([w_mem_k[l], w_mem_v[l]], axis=1).astype(bf16)])
        mkv_out.append(mkv.reshape(B, M, 2, MEM_HEADS, D // MEM_HEADS))

        z, xbc, q, small, kvt, wint = _norm_proj(xp, norm_mix_pre[l], [w_z, w_xbc, w_q, w_small], [w_kv.T, w_win.T], seq=L,
                                                 row_dtypes=[f32, f32, bf16, f32])
        y_ssd, st_new, cv_new = _ssd_prompt(xbc, z, small, conv_w[l], conv_b[l], dt_bias[l], a_log[l], d_skip[l], ssm_norm[l], B)
        pool, poolt = _cmp_pool_prompt(kvt, wt_full, seg_full)
        o_att = _nsa_prompt(q, small, kvt, wint, pool, poolt, expand_p, gb, H)
        xp = _linear_post(xp, [y_ssd, o_att], [wo[:d_ssm], wo[d_ssm:]], norm_mix_post[l])
        kv_p.append(kvt)
        win_p.append(wint[:, :, L - min(WINDOW, L):])
        ssm_p.append(st_new)
        conv_p.append(cv_new)

        z, xbc, q, small, kvr, winr = _norm_proj(xs, norm_mix_pre[l], [w_z, w_xbc, w_q, w_small, w_kv, w_win])
        y_ssd, st_new, cv_new = _ssd_sample(xbc, z, small, ssm_t, conv_t, l, conv_w[l], conv_b[l], dt_bias[l], a_log[l],
                                            d_skip[l], ssm_norm[l], Lt)
        o_att, wnew = _nsa_sample(q, kvr, winr, small, cache_t, win_t, page_table, l, wt_page[:, :page], seg_page,
                                  expand_s, gb, H, Lt)
        xs = _linear_post(xs, [y_ssd, o_att], [wo[:d_ssm], wo[d_ssm:]], norm_mix_post[l])
        kv_s.append(kvr)
        win_s.append(wnew)
        ssm_s.append(st_new)
        conv_s.append(cv_new)

        wq, wmo = w_mem_q[l].astype(bf16), w_mem_o[l].astype(bf16)
        qm, = _norm_proj(xp, norm_mem_pre[l], [wq], row_dtypes=[bf16])
        xp = _linear_post(xp, [_mem_attn_prompt(qm, mkv, L)], [wmo], norm_mem_post[l])
        qm, = _norm_proj(xs, norm_mem_pre[l], [wq])
        xs = _linear_post(xs, [_mem_attn_sample(qm, cache_mem_kv, l, Lt)], [wmo], norm_mem_post[l])

        wu, wd = w_up[l].astype(bf16), w_down[l].astype(bf16)
        xp = _ffn(xp, norm_ffn_pre[l], wu, wd, norm_ffn_post[l], tm=1024)
        xs = _ffn(xs, norm_ffn_pre[l], wu, wd, norm_ffn_post[l], tm=512)

    kv5 = (4, N_KV, HEAD_DIM)
    w5 = (2, N_KV, HEAD_DIM)
    return (xp.reshape(B, L, D), xs.reshape(S, Lt, D),
            jnp.stack(kv_p).reshape(depth, B, *kv5, L).transpose(0, 1, 5, 2, 3, 4),
            jnp.stack(kv_s).reshape(depth, S, Lt, *kv5),
            jnp.stack(win_p).reshape(depth, B, *w5, min(WINDOW, L)).transpose(0, 1, 5, 2, 3, 4),
            jnp.stack(win_s).reshape(depth, S, *w5, keep).transpose(0, 1, 5, 2, 3, 4),
            jnp.stack(ssm_p), jnp.stack(ssm_s).reshape(depth, H, P, N, S).transpose(0, 4, 1, 2, 3),
            jnp.stack(conv_p), jnp.stack(conv_s).transpose(0, 2, 1, 3),
            jnp.stack(mkv_out))
```
